```python
import math
import jax
import jax.numpy as jnp
from jax import lax
import numpy as np

D_MODEL = 1024
BATCH = 8
SEQ = 4096
DEPTH = 4

GDN_HEADS = 4
GDN_DK = 64
GDN_DV = 64
GDN_CONV = 4
GDN_CHUNK = 64
MOBA_HEADS = 4
MOBA_HD = 64
MOBA_BLOCK = 256
MOBA_TOPK = 3
MOBA_QCHUNK = 32
MLA_HEADS = 4
MLA_Q_LORA = 192
MLA_KV_LORA = 128
MLA_NOPE = 64
MLA_ROPE = 32
MLA_V = 64
MLA_QBLOCK = 128
ROPE_THETA = 10000.0
S5_GROUP = 16
S5_GROUPS = 16
S5_WIDTH = S5_GROUP * S5_GROUPS
S5_STATE = 64
D_FF = 2816
FFN_CONV = 3
EPS = 1e-6

GDN_W = GDN_HEADS * GDN_DV
MOBA_W = MOBA_HEADS * MOBA_HD
MLA_W = MLA_HEADS * MLA_V
D_MIX = GDN_W + MOBA_W + MLA_W + S5_WIDTH
SPLITS = (GDN_HEADS * GDN_DK, GDN_HEADS * GDN_DK, GDN_W, GDN_W, GDN_HEADS, GDN_HEADS,
          MOBA_W, MOBA_W, MOBA_W,
          MLA_Q_LORA, MLA_KV_LORA, MLA_ROPE,
          S5_WIDTH)
N_IN = sum(SPLITS)

kernel_name = 'hymba_style_hybrid_trunk'


def rmsnorm(x, g):
    xf = x.astype(jnp.float32)
    y = xf * lax.rsqrt(jnp.mean(xf * xf, axis=-1, keepdims=True) + EPS)
    return (y * g.astype(jnp.float32)).astype(x.dtype)


def l2norm(x):
    return x * lax.rsqrt(jnp.sum(x * x, axis=-1, keepdims=True) + EPS)


def causal_dwconv(x, w):
    k, c = w.shape
    return lax.conv_general_dilated(x, w[:, None, :].astype(x.dtype), window_strides=(1,),
                                    padding=[(k - 1, 0)], dimension_numbers=('NWC', 'WIO', 'NWC'),
                                    feature_group_count=c)


def rope(t, cos, sin):
    t1, t2 = jnp.split(t, 2, axis=-1)
    return jnp.concatenate([t1 * cos - t2 * sin, t2 * cos + t1 * sin], axis=-1)


def gated_deltanet(q, k, v, z, a, bgate, conv_w, a_log, dt_bias, norm_g):
    f32 = jnp.float32
    b, l, _ = q.shape
    h, dk, dv, c = GDN_HEADS, GDN_DK, GDN_DV, GDN_CHUNK
    n = l // c
    qkv = jax.nn.silu(causal_dwconv(jnp.concatenate([q, k, v], axis=-1), conv_w)).astype(f32)
    q, k, v = jnp.split(qkv, 3, axis=-1)
    q = l2norm(q.reshape(b, l, h, dk)) * (dk ** -0.5)
    k = l2norm(k.reshape(b, l, h, dk))
    v = v.reshape(b, l, h, dv)
    beta = jax.nn.sigmoid(bgate.astype(f32))
    g = -jnp.exp(a_log.astype(f32)) * jax.nn.softplus(a.astype(f32) + dt_bias.astype(f32))

    def chunk(t, d):
        return t.reshape(b, n, c, h, d).transpose(0, 3, 1, 2, 4)

    q, k, v = chunk(q, dk), chunk(k, dk), chunk(v, dv)
    beta = beta.reshape(b, n, c, h).transpose(0, 3, 1, 2)
    gam = jnp.cumsum(g.reshape(b, n, c, h).transpose(0, 3, 1, 2), axis=-1)
    causal = jnp.tril(jnp.ones((c, c), dtype=bool))
    strict = jnp.tril(jnp.ones((c, c), dtype=bool), -1)
    decay = jnp.exp(jnp.where(causal, gam[..., :, None] - gam[..., None, :], -jnp.inf))
    kk = jnp.einsum('bhnid,bhnjd->bhnij', k, k)
    lower = jnp.where(strict, beta[..., :, None] * kk * decay, 0.0)
    tmat = lower + jnp.eye(c, dtype=f32)
    rhs = jnp.concatenate([beta[..., None] * v, (beta * jnp.exp(gam))[..., None] * k], axis=-1)
    sol = lax.linalg.triangular_solve(tmat, rhs, left_side=True, lower=True, unit_diagonal=True)
    u, w = sol[..., :dv], sol[..., dv:]
    attn = jnp.einsum('bhnid,bhnjd->bhnij', q, k) * decay
    q_dec = q * jnp.exp(gam)[..., None]
    k_dec = k * jnp.exp(gam[..., -1:] - gam)[..., None]
    g_last = jnp.exp(gam[..., -1])

    def step(s, inp):
        u_c, w_c, a_c, qd_c, kd_c, gl_c = inp
        v_new = u_c - jnp.einsum('bhck,bhkv->bhcv', w_c, s)
        o = jnp.einsum('bhck,bhkv->bhcv', qd_c, s) + jnp.einsum('bhij,bhjv->bhiv', a_c, v_new)
        s = gl_c[..., None, None] * s + jnp.einsum('bhck,bhcv->bhkv', kd_c, v_new)
        return s, o

    xs = (jnp.moveaxis(u, 2, 0), jnp.moveaxis(w, 2, 0), jnp.moveaxis(attn, 2, 0),
          jnp.moveaxis(q_dec, 2, 0), jnp.moveaxis(k_dec, 2, 0), jnp.moveaxis(g_last, 2, 0))
    _, o = lax.scan(step, jnp.zeros((b, h, dk, dv), f32), xs)
    o = o.transpose(1, 0, 3, 2, 4).reshape(b, l, h, dv)
    o = o * lax.rsqrt(jnp.mean(o * o, axis=-1, keepdims=True) + EPS) * norm_g.astype(f32)
    o = o * jax.nn.silu(z.astype(f32).reshape(b, l, h, dv))
    return o.reshape(b, l, h * dv).astype(z.dtype)


def moba_attention(q, k, v):
    f32 = jnp.float32
    b, l, h, d = q.shape
    bs, qc = MOBA_BLOCK, MOBA_QCHUNK
    nb = -(-l // bs)
    lp = nb * bs
    padw = ((0, 0), (0, lp - l), (0, 0), (0, 0))
    qh = jnp.pad(q, padw).transpose(0, 2, 1, 3)
    kb = jnp.pad(k, padw).transpose(0, 2, 1, 3).reshape(b, h, nb, bs, d)
    vb = jnp.pad(v, padw).transpose(0, 2, 1, 3).reshape(b, h, nb, bs, d)
    kmean = jnp.mean(kb.astype(f32), axis=3)
    topk = min(MOBA_TOPK, nb - 1)
    scale = d ** -0.5
    bi = jnp.arange(b)[:, None, None, None]
    hi = jnp.arange(h)[None, :, None, None]

    def query_chunk(ci):
        start = ci * qc
        own = start // bs
        q_c = lax.dynamic_slice_in_dim(qh, start, qc, axis=2)
        k_own = lax.dynamic_index_in_dim(kb, own, axis=2, keepdims=False)
        v_own = lax.dynamic_index_in_dim(vb, own, axis=2, keepdims=False)
        qpos = start + jnp.arange(qc)
        kpos = own * bs + jnp.arange(bs)
        s_own = jnp.einsum('bhqd,bhjd->bhqj', q_c, k_own).astype(f32) * scale
        s_own = jnp.where(kpos[None, :] <= qpos[:, None], s_own, -jnp.inf)
        if topk == 0:
            p = jax.nn.softmax(s_own, axis=-1).astype(v.dtype)
            return jnp.einsum('bhqj,bhjd->bhqd', p, v_own)
        gate = jnp.einsum('bhqd,bhnd->bhqn', q_c.astype(f32), kmean)
        gate = jnp.where(jnp.arange(nb) < own, gate, -jnp.inf)
        _, idx = lax.top_k(gate, topk)
        k_sel = kb[bi, hi, idx]
        v_sel = vb[bi, hi, idx]
        s_sel = jnp.einsum('bhqd,bhqnjd->bhqnj', q_c, k_sel).astype(f32) * scale
        s_sel = jnp.where((jnp.arange(topk) < own)[:, None], s_sel, -jnp.inf)
        s = jnp.concatenate([s_sel.reshape(b, h, qc, topk * bs), s_own], axis=-1)
        p = jax.nn.softmax(s, axis=-1).astype(v.dtype)
        p_sel = p[..., :topk * bs].reshape(b, h, qc, topk, bs)
        return (jnp.einsum('bhqnj,bhqnjd->bhqd', p_sel, v_sel)
                + jnp.einsum('bhqj,bhjd->bhqd', p[..., topk * bs:], v_own))

    o = lax.map(query_chunk, jnp.arange(lp // qc))
    o = o.transpose(1, 0, 3, 2, 4).reshape(b, lp, h, d)[:, :l]
    return o.reshape(b, l, h * d)


def mla_attention(c_q, c_kv, k_rope, q_norm_g, w_uq, kv_norm_g, w_ukv, cos, sin):
    f32 = jnp.float32
    b, l, _ = c_q.shape
    h = MLA_HEADS
    q = (rmsnorm(c_q, q_norm_g) @ w_uq).reshape(b, l, h, MLA_NOPE + MLA_ROPE)
    q_nope, q_rot = q[..., :MLA_NOPE], rope(q[..., MLA_NOPE:], cos[:, None, :], sin[:, None, :])
    kv = (rmsnorm(c_kv, kv_norm_g) @ w_ukv).reshape(b, l, h, MLA_NOPE + MLA_V)
    k_nope, v = kv[..., :MLA_NOPE], kv[..., MLA_NOPE:]
    k_rot = rope(k_rope, cos, sin)
    scale = (MLA_NOPE + MLA_ROPE) ** -0.5
    kpos = jnp.arange(l)

    def q_block(i):
        start = i * MLA_QBLOCK
        qn = lax.dynamic_slice_in_dim(q_nope, start, MLA_QBLOCK, axis=1)
        qr = lax.dynamic_slice_in_dim(q_rot, start, MLA_QBLOCK, axis=1)
        s = (jnp.einsum('bqhd,bkhd->bhqk', qn, k_nope)
             + jnp.einsum('bqhd,bkd->bhqk', qr, k_rot)).astype(f32) * scale
        qpos = start + jnp.arange(MLA_QBLOCK)
        s = jnp.where(kpos[None, :] <= qpos[:, None], s, -jnp.inf)
        p = jax.nn.softmax(s, axis=-1).astype(v.dtype)
        return jnp.einsum('bhqk,bkhd->bqhd', p, v)

    o = lax.map(q_block, jnp.arange(l // MLA_QBLOCK))
    return o.transpose(1, 0, 2, 3, 4).reshape(b, l, h * MLA_V)


def s5_ssm(u, lam_re, lam_im, b_re, b_im, c_re, c_im, d, log_dt, glu_w, glu_b):
    f32 = jnp.float32
    bsz, l, _ = u.shape
    uf = u.astype(f32).reshape(bsz, l, S5_GROUPS, S5_GROUP)
    lr = jnp.minimum(lam_re.astype(f32), -1e-4)
    li = lam_im.astype(f32)
    dt = jnp.exp(log_dt.astype(f32))[:, None]
    mag = jnp.exp(lr * dt)
    ar, ai = mag * jnp.cos(li * dt), mag * jnp.sin(li * dt)
    den = lr * lr + li * li
    cr = ((ar - 1.0) * lr + ai * li) / den
    ci = (ai * lr - (ar - 1.0) * li) / den
    br, bim = b_re.astype(f32), b_im.astype(f32)
    bbr = cr[..., None] * br - ci[..., None] * bim
    bbi = cr[..., None] * bim + ci[..., None] * br
    bur = jnp.einsum('blgc,gpc->blgp', uf, bbr)
    bui = jnp.einsum('blgc,gpc->blgp', uf, bbi)
    a_r = jnp.broadcast_to(ar, (1, l, S5_GROUPS, S5_STATE))
    a_i = jnp.broadcast_to(ai, (1, l, S5_GROUPS, S5_STATE))

    def combine(e1, e2):
        a1r, a1i, b1r, b1i = e1
        a2r, a2i, b2r, b2i = e2
        return (a2r * a1r - a2i * a1i, a2r * a1i + a2i * a1r,
                a2r * b1r - a2i * b1i + b2r, a2r * b1i + a2i * b1r + b2i)

    _, _, xr, xi = lax.associative_scan(combine, (a_r, a_i, bur, bui), axis=1)
    y = (jnp.einsum('blgp,gcp->blgc', xr, c_re.astype(f32))
         - jnp.einsum('blgp,gcp->blgc', xi, c_im.astype(f32))
         + d.astype(f32) * uf)
    y = jax.nn.gelu(y.reshape(bsz, l, S5_WIDTH))
    y = y * jax.nn.sigmoid(y @ glu_w.astype(f32) + glu_b.astype(f32))
    return y.astype(u.dtype)


def hybrid_layer(x, cos, sin, norm1_g, w_in, gdn_conv_w, gdn_a_log, gdn_dt_bias, gdn_norm_g,
                 mla_q_norm_g, mla_w_uq, mla_kv_norm_g, mla_w_ukv,
                 s5_lam_re, s5_lam_im, s5_b_re, s5_b_im, s5_c_re, s5_c_im, s5_d, s5_log_dt,
                 s5_glu_w, s5_glu_b, w_out, norm2_g, ffn_w_up, ffn_conv_w, ffn_w_down):
    b, l, _ = x.shape
    h = rmsnorm(x, norm1_g)
    proj = h @ w_in
    offs = np.cumsum(SPLITS)[:-1].tolist()
    (a_q, a_k, a_v, a_z, a_a, a_b, m_q, m_k, m_v, c_q, c_kv, c_kr, s_u) = jnp.split(proj, offs, axis=-1)
    o_a = gated_deltanet(a_q, a_k, a_v, a_z, a_a, a_b, gdn_conv_w, gdn_a_log, gdn_dt_bias, gdn_norm_g)
    o_b = moba_attention(m_q.reshape(b, l, MOBA_HEADS, MOBA_HD), m_k.reshape(b, l, MOBA_HEADS, MOBA_HD),
                         m_v.reshape(b, l, MOBA_HEADS, MOBA_HD))
    o_c = mla_attention(c_q, c_kv, c_kr, mla_q_norm_g, mla_w_uq, mla_kv_norm_g, mla_w_ukv, cos, sin)
    o_d = s5_ssm(s_u, s5_lam_re, s5_lam_im, s5_b_re, s5_b_im, s5_c_re, s5_c_im, s5_d, s5_log_dt,
                 s5_glu_w, s5_glu_b)
    x = x + jnp.concatenate([o_a, o_b, o_c, o_d], axis=-1) @ w_out
    h = rmsnorm(x, norm2_g)
    gate, val = jnp.split(causal_dwconv(h @ ffn_w_up, ffn_conv_w), 2, axis=-1)
    return x + (jax.nn.silu(gate) * val) @ ffn_w_down


def setup_inputs(seed: int = 0) -> dict:
    key = jax.random.key(seed)
    ks = jax.random.split(key, 32)
    f32 = jnp.float32

    def nrm(i, shape, scale):
        return scale * jax.random.normal(ks[i], shape, f32)

    def gain(i, shape):
        return 1.0 + nrm(i, shape, 0.02)

    dt_gdn = jnp.exp(jax.random.uniform(ks[4], (DEPTH, GDN_HEADS), f32, math.log(1e-3), math.log(1e-1)))
    return {
        'x': nrm(0, (BATCH, SEQ, D_MODEL), 1.0),
        'norm1_g': gain(1, (DEPTH, D_MODEL)),
        'w_in': nrm(2, (DEPTH, D_MODEL, N_IN), D_MODEL ** -0.5),
        'gdn_conv_w': nrm(3, (DEPTH, GDN_CONV, 3 * GDN_W), GDN_CONV ** -0.5),
        'gdn_a_log': jnp.log(jax.random.uniform(ks[5], (DEPTH, GDN_HEADS), f32, 1.0, 16.0)),
        'gdn_dt_bias': dt_gdn + jnp.log(-jnp.expm1(-dt_gdn)),
        'gdn_norm_g': gain(6, (DEPTH, GDN_DV)),
        'mla_q_norm_g': gain(7, (DEPTH, MLA_Q_LORA)),
        'mla_w_uq': nrm(8, (DEPTH, MLA_Q_LORA, MLA_HEADS * (MLA_NOPE + MLA_ROPE)), MLA_Q_LORA ** -0.5),
        'mla_kv_norm_g': gain(9, (DEPTH, MLA_KV_LORA)),
        'mla_w_ukv': nrm(10, (DEPTH, MLA_KV_LORA, MLA_HEADS * (MLA_NOPE + MLA_V)), MLA_KV_LORA ** -0.5),
        's5_lam_re': -0.5 + nrm(11, (DEPTH, S5_GROUPS, S5_STATE), 0.01),
        's5_lam_im': jnp.pi * jnp.arange(S5_STATE, dtype=f32) + nrm(12, (DEPTH, S5_GROUPS, S5_STATE), 0.01),
        's5_b_re': nrm(13, (DEPTH, S5_GROUPS, S5_STATE, S5_GROUP), (2 * S5_GROUP) ** -0.5),
        's5_b_im': nrm(14, (DEPTH, S5_GROUPS, S5_STATE, S5_GROUP), (2 * S5_GROUP) ** -0.5),
        's5_c_re': nrm(15, (DEPTH, S5_GROUPS, S5_GROUP, S5_STATE), S5_STATE ** -0.5),
        's5_c_im': nrm(16, (DEPTH, S5_GROUPS, S5_GROUP, S5_STATE), S5_STATE ** -0.5),
        's5_d': nrm(17, (DEPTH, S5_GROUPS, S5_GROUP), 1.0),
        's5_log_dt': jax.random.uniform(ks[18], (DEPTH, S5_GROUPS), f32, math.log(1e-3), math.log(1e-1)),
        's5_glu_w': nrm(19, (DEPTH, S5_WIDTH, S5_WIDTH), S5_WIDTH ** -0.5),
        's5_glu_b': nrm(20, (DEPTH, S5_WIDTH), 0.01),
        'w_out': nrm(21, (DEPTH, D_MIX, D_MODEL), D_MIX ** -0.5),
        'norm2_g': gain(22, (DEPTH, D_MODEL)),
        'ffn_w_up': nrm(23, (DEPTH, D_MODEL, 2 * D_FF), D_MODEL ** -0.5),
        'ffn_conv_w': nrm(24, (DEPTH, FFN_CONV, 2 * D_FF), FFN_CONV ** -0.5),
        'ffn_w_down': nrm(25, (DEPTH, D_FF, D_MODEL), D_FF ** -0.5),
        'final_norm_g': gain(26, (D_MODEL,)),
    }


def reference(x, norm1_g, w_in, gdn_conv_w, gdn_a_log, gdn_dt_bias, gdn_norm_g,
              mla_q_norm_g, mla_w_uq, mla_kv_norm_g, mla_w_ukv,
              s5_lam_re, s5_lam_im, s5_b_re, s5_b_im, s5_c_re, s5_c_im, s5_d, s5_log_dt,
              s5_glu_w, s5_glu_b, w_out, norm2_g, ffn_w_up, ffn_conv_w, ffn_w_down, final_norm_g):
    l = x.shape[1]
    inv_freq = ROPE_THETA ** (-jnp.arange(0, MLA_ROPE, 2, dtype=jnp.float32) / MLA_ROPE)
    ang = jnp.arange(l, dtype=jnp.float32)[:, None] * inv_freq[None, :]
    cos = jnp.cos(ang).astype(x.dtype)
    sin = jnp.sin(ang).astype(x.dtype)
    for i in range(DEPTH):
        x = hybrid_layer(x, cos, sin, norm1_g[i], w_in[i], gdn_conv_w[i], gdn_a_log[i], gdn_dt_bias[i],
                         gdn_norm_g[i], mla_q_norm_g[i], mla_w_uq[i], mla_kv_norm_g[i], mla_w_ukv[i],
                         s5_lam_re[i], s5_lam_im[i], s5_b_re[i], s5_b_im[i], s5_c_re[i], s5_c_im[i],
                         s5_d[i], s5_log_dt[i], s5_glu_w[i], s5_glu_b[i], w_out[i], norm2_g[i],
                         ffn_w_up[i], ffn_conv_w[i], ffn_w_down[i])
    return rmsnorm(x, final_norm_g)
```

```python
import functools
import math

import jax
import jax.numpy as jnp
from jax import lax
from jax.experimental import pallas as pl
from jax.experimental.pallas import tpu as pltpu

F32 = jnp.float32
BF16 = jnp.bfloat16
NEG_INF = float("-inf")
EPS = 1e-6
HIGHEST = lax.Precision.HIGHEST

D_MODEL = 1024
GDN_HEADS, GDN_DK, GDN_DV, GDN_CONV, GDN_CHUNK = 4, 64, 64, 4, 64
MOBA_HEADS, MOBA_HD, MOBA_BLOCK, MOBA_TOPK = 4, 64, 256, 3
MLA_HEADS, MLA_Q_LORA, MLA_KV_LORA, MLA_NOPE, MLA_ROPE, MLA_V = 4, 192, 128, 64, 32, 64
ROPE_THETA = 10000.0
S5_GROUP, S5_GROUPS, S5_STATE = 16, 16, 64
S5_WIDTH = S5_GROUP * S5_GROUPS
D_FF, FFN_CONV = 2816, 3

LANES = 128
SUBLANES = 8
VMEM_LIMIT = 56 * 1024 * 1024

COL_GQKV, COL_MQKV, COL_AZ, COL_SU, COL_CQ, COL_CKV, COL_SMA, COL_SMB = (
    0, 768, 1536, 1792, 2048, 2304, 2432, 2560)
N_PROJ = 2688
S5_CHUNK = 64


def _cparams(*sems):
    return pltpu.CompilerParams(dimension_semantics=sems, vmem_limit_bytes=VMEM_LIMIT)


def _dot(a, b):
    return jnp.dot(a.astype(BF16), b.astype(BF16), preferred_element_type=F32)


def _dot_nt(a, b):
    return lax.dot_general(a.astype(BF16), b.astype(BF16), (((1,), (1,)), ((), ())),
                           preferred_element_type=F32)


def _sigmoid(x):
    return jax.nn.sigmoid(x)


def _silu(x):
    return x * jax.nn.sigmoid(x)


def _rms_matmul_body(x_ref, g_ref, w_ref, o_ref, *, inv_dim):
    x = x_ref[...]
    ms = jnp.sum(x * x, axis=-1, keepdims=True) * inv_dim
    h = x * lax.rsqrt(ms + EPS) * g_ref[...]
    o_ref[...] = _dot(h, w_ref[...]).astype(o_ref.dtype)


def rms_matmul(x, g, w, tm):
    t, k = x.shape
    n = w.shape[1]
    return pl.pallas_call(
        functools.partial(_rms_matmul_body, inv_dim=1.0 / k),
        grid=(t // tm,),
        in_specs=[pl.BlockSpec((tm, k), lambda i: (i, 0)),
                  pl.BlockSpec((1, k), lambda i: (0, 0)),
                  pl.BlockSpec((k, n), lambda i: (0, 0))],
        out_specs=pl.BlockSpec((tm, n), lambda i: (i, 0)),
        out_shape=jax.ShapeDtypeStruct((t, n), F32),
        compiler_params=_cparams("parallel"),
        name="rms_matmul",
    )(x, g.reshape(1, k), w)


def _rms_body(x_ref, g_ref, o_ref, *, inv_dim):
    x = x_ref[...]
    ms = jnp.sum(x * x, axis=-1, keepdims=True) * inv_dim
    o_ref[...] = x * lax.rsqrt(ms + EPS) * g_ref[...]


def rmsnorm_rows(x, g, tm):
    t, k = x.shape
    return pl.pallas_call(
        functools.partial(_rms_body, inv_dim=1.0 / k),
        grid=(t // tm,),
        in_specs=[pl.BlockSpec((tm, k), lambda i: (i, 0)),
                  pl.BlockSpec((1, k), lambda i: (0, 0))],
        out_specs=pl.BlockSpec((tm, k), lambda i: (i, 0)),
        out_shape=jax.ShapeDtypeStruct((t, k), F32),
        compiler_params=_cparams("parallel"),
        name="final_rmsnorm",
    )(x, g.reshape(1, k))


def _gdn_prep_body(qkv_ref, halo_ref, sm_ref, cw_ref, alog_ref, dtb_ref, tri_ref,
                   q_ref, k_ref, v_ref, beta_ref, gam_ref, xs_ref, *, tm):
    i = pl.program_id(1)
    keep = (i > 0).astype(F32)
    xs_ref[0:SUBLANES, :] = halo_ref[0] * keep
    xs_ref[SUBLANES:, :] = qkv_ref[0]
    w = cw_ref[...]
    y = w[3:4, :] * xs_ref[pl.ds(SUBLANES, tm), :]
    for tap in range(GDN_CONV - 1):
        y = y + w[tap:tap + 1, :] * xs_ref[pl.ds(SUBLANES - (GDN_CONV - 1) + tap, tm), :]
    y = _silu(y)
    sm = sm_ref[0]
    g = -jnp.exp(alog_ref[...]) * jax.nn.softplus(sm + dtb_ref[...])
    beta = _sigmoid(sm)
    gam = jnp.dot(tri_ref[...], g, precision=HIGHEST, preferred_element_type=F32)
    nh, dk = GDN_HEADS, GDN_DK
    for h in range(nh):
        qh = y[:, h * dk:(h + 1) * dk]
        kh = y[:, nh * dk + h * dk: nh * dk + (h + 1) * dk]
        vh = y[:, 2 * nh * dk + h * dk: 2 * nh * dk + (h + 1) * dk]
        q_ref[0, h] = qh * lax.rsqrt(jnp.sum(qh * qh, axis=-1, keepdims=True) + EPS) * (dk ** -0.5)
        k_ref[0, h] = kh * lax.rsqrt(jnp.sum(kh * kh, axis=-1, keepdims=True) + EPS)
        v_ref[0, h] = vh
        beta_ref[0, h] = jnp.broadcast_to(beta[:, nh + h: nh + h + 1], (tm, dk))
        gam_ref[0, h] = jnp.broadcast_to(gam[:, h:h + 1], (tm, dk))


def gdn_prep(proj3, conv_w, a_log, dt_bias, tm):
    b, l, _ = proj3.shape
    nh, dk = GDN_HEADS, GDN_DK
    wq = 3 * nh * dk
    alog = jnp.zeros((1, LANES), F32).at[0, :nh].set(a_log)
    dtb = jnp.zeros((1, LANES), F32).at[0, :nh].set(dt_bias)
    r = jnp.arange(tm)
    tri = ((r[:, None] // GDN_CHUNK == r[None, :] // GDN_CHUNK) & (r[None, :] <= r[:, None])).astype(F32)
    hm = jax.ShapeDtypeStruct((b, nh, l, dk), F32)
    hm_spec = pl.BlockSpec((1, nh, tm, dk), lambda bi, i: (bi, 0, i, 0))
    halo_blocks = tm // SUBLANES
    return pl.pallas_call(
        functools.partial(_gdn_prep_body, tm=tm),
        grid=(b, l // tm),
        in_specs=[pl.BlockSpec((1, tm, wq), lambda bi, i: (bi, i, COL_GQKV // wq)),
                  pl.BlockSpec((1, SUBLANES, wq),
                               lambda bi, i: (bi, jnp.maximum(i * halo_blocks - 1, 0), COL_GQKV // wq)),
                  pl.BlockSpec((1, tm, LANES), lambda bi, i: (bi, i, COL_SMA // LANES)),
                  pl.BlockSpec((GDN_CONV, wq), lambda bi, i: (0, 0)),
                  pl.BlockSpec((1, LANES), lambda bi, i: (0, 0)),
                  pl.BlockSpec((1, LANES), lambda bi, i: (0, 0)),
                  pl.BlockSpec((tm, tm), lambda bi, i: (0, 0))],
        out_specs=[hm_spec] * 5,
        out_shape=[hm] * 5,
        scratch_shapes=[pltpu.VMEM((tm + SUBLANES, wq), F32)],
        compiler_params=_cparams("parallel", "arbitrary"),
        name="gdn_prep",
    )(proj3, proj3, proj3, conv_w, alog, dtb, tri)


def _split_bf16(a):
    hi = a.astype(BF16)
    lo = (a - hi.astype(F32)).astype(BF16)
    return hi, lo


def _bmm3(a, b):
    ah, al = _split_bf16(a)
    bh, bl = _split_bf16(b)
    f = lambda u, v: jnp.einsum("nij,njk->nik", u, v, preferred_element_type=F32)
    return f(ah, bh) + f(ah, bl) + f(al, bh)


def _gdn_chunk_body(q_ref, k_ref, v_ref, beta_ref, gam_ref, gr_ref, z_ref, ng_ref, o_ref, s_ref,
                    *, nc):
    c = GDN_CHUNK
    nh, dk, dv = GDN_HEADS, GDN_DK, GDN_DV

    @pl.when(pl.program_id(1) == 0)
    def _():
        s_ref[...] = jnp.zeros_like(s_ref)

    rows = lax.broadcasted_iota(jnp.int32, (nc, c, c), 1)
    cols = lax.broadcasted_iota(jnp.int32, (nc, c, c), 2)
    causal = rows >= cols
    strict = rows > cols
    eye = (rows == cols).astype(F32)
    ng = ng_ref[...]
    for h in range(nh):
        q = q_ref[0, h].reshape(nc, c, dk)
        k = k_ref[0, h].reshape(nc, c, dk)
        v = v_ref[0, h].reshape(nc, c, dv)
        beta = beta_ref[0, h].reshape(nc, c, c)
        gc = gam_ref[0, h].reshape(nc, c, c)
        gr = jnp.broadcast_to(gr_ref[0, h], (nc, c, c))
        decay = jnp.exp(jnp.where(causal, gc - gr, NEG_INF))
        kb = k.astype(BF16)
        kk = jnp.einsum("nid,njd->nij", kb, kb, preferred_element_type=F32)
        x = -jnp.where(strict, beta * kk * decay, 0.0)
        p = eye + x
        for _ in range(5):
            x = _bmm3(x, x)
            p = p + _bmm3(p, x)
        eg = jnp.exp(gc)
        u = _bmm3(p, beta * v)
        w = _bmm3(p, beta * eg * k)
        attn = jnp.einsum("nid,njd->nij", q.astype(BF16), kb, preferred_element_type=F32) * decay
        q_dec = q * eg
        g_last = gc[:, c - 1:c, :]
        k_dec = k * jnp.exp(g_last - gc)
        gl = jnp.broadcast_to(jnp.exp(g_last), (nc, c, c))
        s = s_ref[h]
        for n in range(nc):
            sb = s.astype(BF16)
            v_new = u[n] - _dot(w[n], sb)
            o = _dot(q_dec[n], sb) + _dot(attn[n], v_new)
            s = gl[n] * s + lax.dot_general(k_dec[n].astype(BF16), v_new.astype(BF16),
                                            (((0,), (0,)), ((), ())), preferred_element_type=F32)
            o = o * lax.rsqrt(jnp.mean(o * o, axis=-1, keepdims=True) + EPS) * ng
            o_ref[0, h, n * c:(n + 1) * c, :] = o * _silu(z_ref[0, h, n * c:(n + 1) * c, :])
        s_ref[h] = s


def gdn_chunk(q, k, v, beta, gam, gr, z, norm_g, lt):
    b, nh, l, dk = q.shape
    nc = lt // GDN_CHUNK
    hm_spec = pl.BlockSpec((1, nh, lt, dk), lambda bi, i: (bi, 0, i, 0))
    return pl.pallas_call(
        functools.partial(_gdn_chunk_body, nc=nc),
        grid=(b, l // lt),
        in_specs=[hm_spec] * 5 + [
            pl.BlockSpec((1, nh, nc, 1, GDN_CHUNK), lambda bi, i: (bi, 0, i, 0, 0)),
            hm_spec,
            pl.BlockSpec((1, GDN_DV), lambda bi, i: (0, 0))],
        out_specs=hm_spec,
        out_shape=jax.ShapeDtypeStruct((b, nh, l, GDN_DV), F32),
        scratch_shapes=[pltpu.VMEM((nh, GDN_DK, GDN_DV), F32)],
        compiler_params=_cparams("parallel", "arbitrary"),
        name="gdn_chunk",
    )(q, k, v, beta, gam, gr, z, norm_g.reshape(1, GDN_DV))


def _moba_gate_body(q_ref, k_ref, sel_ref, km_ref, *, nb):
    i = pl.program_id(1)
    nh, hd = MOBA_HEADS, MOBA_HD
    w = nh * hd

    assert nb & (nb - 1) == 0 and hd & (hd - 1) == 0
    nb_bits, hd_bits = nb.bit_length() - 1, hd.bit_length() - 1

    @pl.when(i == 0)
    def _():
        kmean = jnp.mean(k_ref[0].reshape(nb, MOBA_BLOCK, w), axis=1)
        lane = lax.broadcasted_iota(jnp.int32, (nb, w), 1)
        for h in range(nh):
            km_ref[h * nb:(h + 1) * nb, :] = jnp.where(jnp.right_shift(lane, hd_bits) == h, kmean, 0.0)

    gate = lax.dot_general(q_ref[0], km_ref[...], (((1,), (1,)), ((), ())),
                           precision=HIGHEST, preferred_element_type=F32)
    lane = lax.broadcasted_iota(jnp.int32, gate.shape, 1)
    valid = jnp.bitwise_and(lane, nb - 1) < i
    big = jnp.int32(nh * nb)
    sel = jnp.zeros(gate.shape, F32)
    for h in range(nh):
        avail = valid & (jnp.right_shift(lane, nb_bits) == h)
        for _ in range(MOBA_TOPK):
            m = jnp.max(jnp.where(avail, gate, NEG_INF), axis=-1, keepdims=True)
            cand = avail & (gate == m)
            first = jnp.min(jnp.where(cand, lane, big), axis=-1, keepdims=True)
            pick = lane == first
            sel = jnp.where(pick, 1.0, sel)
            avail = avail & jnp.logical_not(pick)
    sel_ref[0] = sel


def moba_gate(proj3):
    b, l, _ = proj3.shape
    w = MOBA_HEADS * MOBA_HD
    nb = l // MOBA_BLOCK
    tq = MOBA_BLOCK
    return pl.pallas_call(
        functools.partial(_moba_gate_body, nb=nb),
        grid=(b, l // tq),
        in_specs=[pl.BlockSpec((1, tq, w), lambda bi, i: (bi, i, COL_MQKV // w)),
                  pl.BlockSpec((1, l, w), lambda bi, i: (bi, 0, COL_MQKV // w + 1))],
        out_specs=pl.BlockSpec((1, tq, MOBA_HEADS * nb), lambda bi, i: (bi, i, 0)),
        out_shape=jax.ShapeDtypeStruct((b, l, MOBA_HEADS * nb), F32),
        scratch_shapes=[pltpu.VMEM((MOBA_HEADS * nb, w), F32)],
        compiler_params=_cparams("parallel", "arbitrary"),
        name="moba_gate",
    )(proj3, proj3)


def _flash_body(*refs, nh, tq, paired, q_scale, nb):
    use_sel = len(refs) == 8
    if use_sel:
        q_ref, k_ref, v_ref, sel_ref, o_ref, m_scr, l_scr, acc_scr = refs
    else:
        q_ref, k_ref, v_ref, o_ref, m_scr, l_scr, acc_scr = refs
    i = pl.program_id(1)
    r0 = pl.multiple_of(i * tq, tq)
    row = lax.broadcasted_iota(jnp.int32, (tq, tq), 0)
    col = lax.broadcasted_iota(jnp.int32, (tq, tq), 1)
    tri = col <= row
    lane = lax.broadcasted_iota(jnp.int32, (tq, LANES), 1)
    low = lane < LANES // 2
    outs = []
    for h in range(nh):
        qk_lo = (h // 2) * LANES if paired else h * LANES
        v_lo = (h // 2) * LANES
        q = q_ref[0, :, qk_lo:qk_lo + LANES]
        if paired:
            q = jnp.where(low if h % 2 == 0 else jnp.logical_not(low), q, jnp.zeros_like(q))
        if q_scale != 1.0:
            q = q * q_scale

        def tile(kv_r0, bias, first):
            kt = k_ref[0, pl.ds(kv_r0, tq), qk_lo:qk_lo + LANES]
            vt = v_ref[0, pl.ds(kv_r0, tq), v_lo:v_lo + LANES]
            s = _dot_nt(q, kt)
            if first:
                s = jnp.where(tri, s, NEG_INF)
                m_prev = jnp.full((tq, LANES), NEG_INF, F32)
            else:
                if bias is not None:
                    s = s + bias
                m_prev = m_scr[...]
            m_cur = jnp.max(s, axis=1, keepdims=True)
            m_next = jnp.maximum(m_prev, m_cur)
            p = jnp.exp(s - jnp.concatenate([m_next] * (tq // LANES), axis=1))
            psum = jnp.sum(p, axis=1, keepdims=True)
            pv = _dot(p, vt)
            if first:
                l_scr[...] = jnp.broadcast_to(psum, (tq, LANES))
                acc_scr[...] = pv
            else:
                alpha = jnp.exp(m_prev - m_next)
                l_scr[...] = alpha * l_scr[...] + psum
                acc_scr[...] = alpha * acc_scr[...] + pv
            m_scr[...] = m_next

        tile(r0, None, True)

        def past(j, carry):
            bias = None
            if use_sel:
                onehot = (lax.broadcasted_iota(jnp.int32, (1, nh * nb), 1) == h * nb + j).astype(F32)
                hit = jnp.sum(sel_ref[0] * onehot, axis=1, keepdims=True)
                bias = jnp.where(hit > 0.5, 0.0, NEG_INF)
            tile(pl.multiple_of(j * tq, tq), bias, False)
            return carry

        lax.fori_loop(0, i, past, 0)
        outs.append(acc_scr[...] / l_scr[...])
    pairs = [jnp.where(low, outs[2 * p], outs[2 * p + 1]) for p in range(nh // 2)]
    o_ref[0] = jnp.concatenate(pairs, axis=1)


def flash_attention(q, k, v, sel, *, nh, paired, q_scale):
    b, l, wq = q.shape
    wv = v.shape[2]
    tq = MOBA_BLOCK
    nb = l // tq
    in_specs = [pl.BlockSpec((1, tq, wq), lambda bi, i: (bi, i, 0)),
                pl.BlockSpec((1, l, wq), lambda bi, i: (bi, 0, 0)),
                pl.BlockSpec((1, l, wv), lambda bi, i: (bi, 0, 0))]
    args = [q, k, v]
    if sel is not None:
        in_specs.append(pl.BlockSpec((1, tq, nh * nb), lambda bi, i: (bi, i, 0)))
        args.append(sel)
    return pl.pallas_call(
        functools.partial(_flash_body, nh=nh, tq=tq, paired=paired, q_scale=q_scale, nb=nb),
        grid=(b, l // tq),
        in_specs=in_specs,
        out_specs=pl.BlockSpec((1, tq, wv), lambda bi, i: (bi, i, 0)),
        out_shape=jax.ShapeDtypeStruct((b, l, wv), F32),
        scratch_shapes=[pltpu.VMEM((tq, LANES), F32)] * 3,
        compiler_params=_cparams("parallel", "arbitrary"),
        name="flash_moba" if sel is not None else "flash_mla",
    )(*args)


def _mla_proj_body(cq_ref, ckv_ref, sma_ref, smb_ref, qg_ref, kvg_ref, wq_ref, wkv_ref,
                   cq_t, sq_t, ck_t, sk_t, q_ref, k_ref, v_ref):
    nh = MLA_HEADS
    cq = cq_ref[0]
    hq = cq * lax.rsqrt(jnp.sum(cq * cq, axis=-1, keepdims=True) * (1.0 / MLA_Q_LORA) + EPS) * qg_ref[...]
    qq = _dot(hq, wq_ref[...])
    half = nh * LANES
    q_ref[0] = (qq[:, :half] * cq_t[...] + qq[:, half:] * sq_t[...]).astype(BF16)
    ckv = ckv_ref[0]
    hkv = ckv * lax.rsqrt(jnp.mean(ckv * ckv, axis=-1, keepdims=True) + EPS) * kvg_ref[...]
    kv = _dot(hkv, wkv_ref[...])
    krot = sma_ref[0] * ck_t[...] + smb_ref[0] * sk_t[...]
    k_ref[0] = (kv[:, :half] + jnp.concatenate([krot] * nh, axis=1)).astype(BF16)
    v_ref[0] = kv[:, half:].astype(BF16)


def mla_proj(proj3, q_norm_g, w_uq, kv_norm_g, w_ukv, cos, sin, tm):
    b, l, _ = proj3.shape
    nh, dn, dr, dvh = MLA_HEADS, MLA_NOPE, MLA_ROPE, MLA_V
    hr = dr // 2
    scale = (dn + dr) ** -0.5
    cqw = COL_CKV - COL_CQ
    w1 = jnp.zeros((cqw, nh, LANES), F32)
    w2 = jnp.zeros((cqw, nh, LANES), F32)
    wq3 = w_uq.reshape(MLA_Q_LORA, nh, dn + dr)
    w1 = w1.at[:MLA_Q_LORA, :, :dn + dr].set(wq3)
    w2 = w2.at[:MLA_Q_LORA, :, dn:dn + hr].set(-wq3[:, :, dn + hr:])
    w2 = w2.at[:MLA_Q_LORA, :, dn + hr:dn + dr].set(wq3[:, :, dn:dn + hr])
    wq = jnp.concatenate([w1.reshape(cqw, nh * LANES), w2.reshape(cqw, nh * LANES)], axis=1).astype(BF16)
    qg = jnp.zeros((1, cqw), F32).at[0, :MLA_Q_LORA].set(q_norm_g)
    wkv3 = w_ukv.reshape(MLA_KV_LORA, nh, dn + dvh)
    wk = jnp.zeros((MLA_KV_LORA, nh, LANES), F32).at[:, :, :dn].set(wkv3[:, :, :dn])
    wkv = jnp.concatenate([wk.reshape(MLA_KV_LORA, nh * LANES),
                           wkv3[:, :, dn:].reshape(MLA_KV_LORA, nh * dvh)], axis=1).astype(BF16)
    cc = jnp.concatenate([cos, cos], axis=1)
    ss = jnp.concatenate([sin, sin], axis=1)
    pad = jnp.zeros((l, LANES - dn - dr), F32)
    cq_tile = jnp.concatenate([jnp.ones((l, dn), F32), cc, pad], axis=1) * scale
    sq_tile = jnp.concatenate([jnp.zeros((l, dn), F32), ss, pad], axis=1) * scale
    cq_t = jnp.tile(cq_tile, (1, nh))
    sq_t = jnp.tile(sq_tile, (1, nh))
    ck_t = jnp.concatenate([jnp.zeros((l, dn), F32), cc, pad], axis=1)
    sk_t = jnp.concatenate([jnp.zeros((l, dn), F32), ss, pad], axis=1)
    row = lambda w_, c_: pl.BlockSpec((1, tm, w_), lambda bi, i: (bi, i, c_ // w_))
    full = lambda a: pl.BlockSpec(a.shape, lambda bi, i: (0, 0))
    tab = lambda w_: pl.BlockSpec((tm, w_), lambda bi, i: (i, 0))
    out_q = jax.ShapeDtypeStruct((b, l, nh * LANES), BF16)
    out_v = jax.ShapeDtypeStruct((b, l, nh * dvh), BF16)
    kvg = kv_norm_g.reshape(1, MLA_KV_LORA)
    return pl.pallas_call(
        _mla_proj_body,
        grid=(b, l // tm),
        in_specs=[row(cqw, COL_CQ), row(MLA_KV_LORA, COL_CKV), row(LANES, COL_SMA), row(LANES, COL_SMB),
                  full(qg), full(kvg), full(wq), full(wkv),
                  tab(nh * LANES), tab(nh * LANES), tab(LANES), tab(LANES)],
        out_specs=[pl.BlockSpec((1, tm, nh * LANES), lambda bi, i: (bi, i, 0)),
                   pl.BlockSpec((1, tm, nh * LANES), lambda bi, i: (bi, i, 0)),
                   pl.BlockSpec((1, tm, nh * dvh), lambda bi, i: (bi, i, 0))],
        out_shape=[out_q, out_q, out_v],
        compiler_params=_cparams("parallel", "parallel"),
        name="mla_proj",
    )(proj3, proj3, proj3, proj3, qg, kvg, wq, wkv, cq_t, sq_t, ck_t, sk_t)


def _s5_body(u_ref, mt_ref, pr_ref, pi_ref, qr_ref, qi_ref, ar_ref, ai_ref, y_ref,
             pur_scr, pui_scr, xr_scr, xi_scr, *, nch, nb):
    u = u_ref[0]
    pur_scr[...] = _dot(u, pr_ref[0])
    pui_scr[...] = _dot(u, pi_ref[0])
    ar = ar_ref[0]
    ai = ai_ref[0]

    def step(kc, carry):
        xr, xi = carry
        r0 = pl.multiple_of(kc * nb, nb)
        xr_scr[pl.ds(r0, nb), :] = xr
        xi_scr[pl.ds(r0, nb), :] = xi
        nr = ar * xr - ai * xi + pur_scr[pl.ds(r0, nb), :]
        ni = ar * xi + ai * xr + pui_scr[pl.ds(r0, nb), :]
        return nr, ni

    zero = jnp.zeros((nb, S5_STATE), F32)
    lax.fori_loop(0, nch, step, (zero, zero))
    y_ref[0] = _dot(u, mt_ref[0]) + _dot(xr_scr[...], qr_ref[0]) + _dot(xi_scr[...], qi_ref[0])


def s5_weights(lam_re, lam_im, b_re, b_im, c_re, c_im, d, log_dt):
    cn = S5_CHUNK
    g, p, cg = S5_GROUPS, S5_STATE, S5_GROUP
    ein = functools.partial(jnp.einsum, precision=HIGHEST)
    lr = jnp.minimum(lam_re, -1e-4)
    li = lam_im
    dt = jnp.exp(log_dt)[:, None]
    mag = jnp.exp(lr * dt)
    ar, ai = mag * jnp.cos(li * dt), mag * jnp.sin(li * dt)
    den = lr * lr + li * li
    cr = ((ar - 1.0) * lr + ai * li) / den
    ci = (ai * lr - (ar - 1.0) * li) / den
    bbr = cr[..., None] * b_re - ci[..., None] * b_im
    bbi = cr[..., None] * b_im + ci[..., None] * b_re
    tau = jnp.arange(cn + 1, dtype=F32)[None, :, None]
    pmag = jnp.exp((lr * dt)[:, None, :] * tau)
    pang = (li * dt)[:, None, :] * tau
    pwr, pwi = pmag * jnp.cos(pang), pmag * jnp.sin(pang)
    clr = c_re[:, None] * pwr[:, :, None, :] - c_im[:, None] * pwi[:, :, None, :]
    cli = c_re[:, None] * pwi[:, :, None, :] + c_im[:, None] * pwr[:, :, None, :]
    kt = ein("gtap,gpc->gtac", clr[:, :cn], bbr) - ein("gtap,gpc->gtac", cli[:, :cn], bbi)
    kt = kt.at[:, 0].add(d[:, :, None] * jnp.eye(cg, dtype=F32)[None])
    s_idx = jnp.arange(cn)[:, None]
    t_idx = jnp.arange(cn)[None, :]
    lag = t_idx - s_idx
    ktoe = jnp.where((lag >= 0)[None, :, :, None, None], kt[:, jnp.clip(lag, 0, cn - 1)], 0.0)
    mt = ktoe.transpose(0, 1, 4, 2, 3).reshape(g, cn * cg, cn * cg)
    rev_r, rev_i = pwr[:, cn - 1::-1][:, :cn], pwi[:, cn - 1::-1][:, :cn]
    pr = (rev_r[:, :, None, :] * bbr.transpose(0, 2, 1)[:, None] - rev_i[:, :, None, :] * bbi.transpose(0, 2, 1)[:, None])
    pi = (rev_r[:, :, None, :] * bbi.transpose(0, 2, 1)[:, None] + rev_i[:, :, None, :] * bbr.transpose(0, 2, 1)[:, None])
    pr = pr.reshape(g, cn * cg, p)
    pi = pi.reshape(g, cn * cg, p)
    qr = clr[:, 1:].transpose(0, 3, 1, 2).reshape(g, p, cn * cg)
    qi = -cli[:, 1:].transpose(0, 3, 1, 2).reshape(g, p, cn * cg)
    acr, aci = pwr[:, cn][:, None, :], pwi[:, cn][:, None, :]
    return mt.astype(BF16), pr.astype(BF16), pi.astype(BF16), qr.astype(BF16), qi.astype(BF16), acr, aci


def s5_scan(u3, weights):
    b, l, _ = u3.shape
    cn, g, cg, p = S5_CHUNK, S5_GROUPS, S5_GROUP, S5_STATE
    nch = l // cn
    wdt = cn * cg
    mt, pr, pi, qr, qi, acr, aci = weights
    ur = u3.astype(BF16).reshape(b, nch, cn, g, cg).transpose(3, 1, 0, 2, 4).reshape(g, nch * b, wdt)
    gspec = lambda shp: pl.BlockSpec((1,) + shp, lambda gi: (gi, 0, 0))
    y = pl.pallas_call(
        functools.partial(_s5_body, nch=nch, nb=b),
        grid=(g,),
        in_specs=[gspec((nch * b, wdt)), gspec((wdt, wdt)), gspec((wdt, p)), gspec((wdt, p)),
                  gspec((p, wdt)), gspec((p, wdt)), gspec((1, p)), gspec((1, p))],
        out_specs=gspec((nch * b, wdt)),
        out_shape=jax.ShapeDtypeStruct((g, nch * b, wdt), F32),
        scratch_shapes=[pltpu.VMEM((nch * b, p), F32)] * 4,
        compiler_params=_cparams("parallel"),
        name="s5_scan",
    )(ur, mt, pr, pi, qr, qi, acr, aci)
    return y.reshape(g, nch, b, cn, cg).transpose(2, 1, 3, 0, 4).reshape(b, l, g * cg)


def _out_proj_body(x_ref, oa_ref, ob_ref, oc_ref, ys_ref, gw_ref, gb_ref, wo_ref, o_ref):
    y = jax.nn.gelu(ys_ref[...])
    od = y * _sigmoid(_dot(y, gw_ref[...]) + gb_ref[...])
    mix = jnp.concatenate([oa_ref[...], ob_ref[...], oc_ref[...], od], axis=1)
    o_ref[...] = x_ref[...] + _dot(mix, wo_ref[...])


def out_proj(x, oa, ob, oc, ys, glu_w, glu_b, w_out, tm):
    t, dm = x.shape
    wm = oa.shape[1]
    row = lambda w_: pl.BlockSpec((tm, w_), lambda i: (i, 0))
    full = lambda a: pl.BlockSpec(a.shape, lambda i: (0, 0))
    gb = glu_b.reshape(1, wm)
    return pl.pallas_call(
        _out_proj_body,
        grid=(t // tm,),
        in_specs=[row(dm), row(wm), row(wm), row(wm), row(wm), full(glu_w), full(gb), full(w_out)],
        out_specs=row(dm),
        out_shape=jax.ShapeDtypeStruct((t, dm), F32),
        compiler_params=_cparams("parallel"),
        name="out_proj",
    )(x, oa, ob, oc, ys, glu_w, gb, w_out)


def _ffn_body(x_ref, halo_ref, g_ref, wup_ref, cw_ref, wdn_ref, o_ref, up_scr, *, tm, fc):
    i = pl.program_id(1)
    keep = (i > 0).astype(F32)
    g = g_ref[...]

    def norm(v):
        return (v * lax.rsqrt(jnp.mean(v * v, axis=-1, keepdims=True) + EPS) * g).astype(BF16)

    x = x_ref[0]
    h_main = norm(x)
    h_halo = norm(halo_ref[0])
    acc = x
    for c in range(D_FF // fc):
        acts = []
        for part in range(2):
            lo = part * D_FF + c * fc
            up_scr[0:SUBLANES, :] = _dot(h_halo, wup_ref[:, lo:lo + fc]) * keep
            up_scr[SUBLANES:, :] = _dot(h_main, wup_ref[:, lo:lo + fc])
            w = cw_ref[:, lo:lo + fc]
            y = w[FFN_CONV - 1:FFN_CONV, :] * up_scr[pl.ds(SUBLANES, tm), :]
            for tap in range(FFN_CONV - 1):
                y = y + w[tap:tap + 1, :] * up_scr[pl.ds(SUBLANES - (FFN_CONV - 1) + tap, tm), :]
            acts.append(y)
        act = _silu(acts[0]) * acts[1]
        acc = acc + _dot(act, wdn_ref[c * fc:(c + 1) * fc, :])
    o_ref[0] = acc


def ffn(x3, norm_g, w_up, conv_w, w_down, tm, fc):
    b, l, dm = x3.shape
    halo_blocks = tm // SUBLANES
    once = lambda a: pl.BlockSpec(a.shape, lambda bi, i: (0, 0), pipeline_mode=pl.Buffered(1))
    g = norm_g.reshape(1, dm)
    return pl.pallas_call(
        functools.partial(_ffn_body, tm=tm, fc=fc),
        grid=(b, l // tm),
        in_specs=[pl.BlockSpec((1, tm, dm), lambda bi, i: (bi, i, 0)),
                  pl.BlockSpec((1, SUBLANES, dm), lambda bi, i: (bi, jnp.maximum(i * halo_blocks - 1, 0), 0)),
                  once(g), once(w_up), once(conv_w), once(w_down)],
        out_specs=pl.BlockSpec((1, tm, dm), lambda bi, i: (bi, i, 0)),
        out_shape=jax.ShapeDtypeStruct((b, l, dm), F32),
        scratch_shapes=[pltpu.VMEM((tm + SUBLANES, fc), F32)],
        compiler_params=_cparams("parallel", "arbitrary"),
        name="ffn",
    )(x3, x3, g, w_up, conv_w, w_down)


def _permute_w_in(w_in):
    dm = w_in.shape[0]
    gw, mw = GDN_HEADS * GDN_DK, MOBA_HEADS * MOBA_HD
    o_aq, o_az, o_aa, o_ab = 0, 3 * gw, 4 * gw, 4 * gw + GDN_HEADS
    o_mq = o_ab + GDN_HEADS
    o_cq = o_mq + 3 * mw
    o_ckv = o_cq + MLA_Q_LORA
    o_ckr = o_ckv + MLA_KV_LORA
    o_su = o_ckr + MLA_ROPE
    hr = MLA_ROPE // 2
    z = lambda n: jnp.zeros((dm, n), w_in.dtype)
    ckr = w_in[:, o_ckr:o_ckr + MLA_ROPE]
    ckr_swap = jnp.concatenate([-ckr[:, hr:], ckr[:, :hr]], axis=1)
    cols = [w_in[:, o_aq:o_aq + 3 * gw],
            w_in[:, o_mq:o_mq + 3 * mw],
            w_in[:, o_az:o_az + gw],
            w_in[:, o_su:o_su + S5_WIDTH],
            w_in[:, o_cq:o_cq + MLA_Q_LORA], z(COL_CKV - COL_CQ - MLA_Q_LORA),
            w_in[:, o_ckv:o_ckv + MLA_KV_LORA],
            w_in[:, o_aa:o_aa + 2 * GDN_HEADS], z(MLA_NOPE - 2 * GDN_HEADS), ckr, z(LANES - MLA_NOPE - MLA_ROPE),
            z(MLA_NOPE), ckr_swap, z(LANES - MLA_NOPE - MLA_ROPE)]
    return jnp.concatenate(cols, axis=1).astype(BF16)


def _layer(x3, cos, sin, p):
    b, l, dm = x3.shape
    t = b * l
    x2 = x3.reshape(t, dm)
    proj = rms_matmul(x2, p["norm1_g"], _permute_w_in(p["w_in"]), 512)
    proj3 = proj.reshape(b, l, N_PROJ)

    q, k, v, beta, gam = gdn_prep(proj3, p["gdn_conv_w"], p["gdn_a_log"], p["gdn_dt_bias"], 512)
    nh = GDN_HEADS
    gr = gam[..., 0].reshape(b, nh, l // GDN_CHUNK, 1, GDN_CHUNK)
    z = proj3[:, :, COL_AZ:COL_AZ + nh * GDN_DV].reshape(b, l, nh, GDN_DV).transpose(0, 2, 1, 3)
    o_a = gdn_chunk(q, k, v, beta, gam, gr, z, p["gdn_norm_g"], 512)
    o_a = o_a.transpose(0, 2, 1, 3).reshape(t, nh * GDN_DV)

    mw = MOBA_HEADS * MOBA_HD
    sel = moba_gate(proj3)
    mqkv = proj3[:, :, COL_MQKV:COL_MQKV + 3 * mw].astype(BF16)
    o_b = flash_attention(mqkv[:, :, :mw], mqkv[:, :, mw:2 * mw], mqkv[:, :, 2 * mw:], sel,
                          nh=MOBA_HEADS, paired=True, q_scale=MOBA_HD ** -0.5).reshape(t, mw)

    cq, ck, cv = mla_proj(proj3, p["mla_q_norm_g"], p["mla_w_uq"], p["mla_kv_norm_g"], p["mla_w_ukv"],
                          cos, sin, 512)
    o_c = flash_attention(cq, ck, cv, None, nh=MLA_HEADS, paired=False, q_scale=1.0).reshape(t, MLA_HEADS * MLA_V)

    s5w = s5_weights(p["s5_lam_re"], p["s5_lam_im"], p["s5_b_re"], p["s5_b_im"], p["s5_c_re"], p["s5_c_im"],
                     p["s5_d"], p["s5_log_dt"])
    ys = s5_scan(proj3[:, :, COL_SU:COL_SU + S5_WIDTH], s5w).reshape(t, S5_WIDTH)

    x2 = out_proj(x2, o_a, o_b, o_c, ys, p["s5_glu_w"].astype(BF16), p["s5_glu_b"], p["w_out"].astype(BF16), 512)
    return ffn(x2.reshape(b, l, dm), p["norm2_g"], p["ffn_w_up"].astype(BF16), p["ffn_conv_w"],
               p["ffn_w_down"].astype(BF16), 512, D_FF // 2)


_LAYER_KEYS = ("norm1_g", "w_in", "gdn_conv_w", "gdn_a_log", "gdn_dt_bias", "gdn_norm_g",
               "mla_q_norm_g", "mla_w_uq", "mla_kv_norm_g", "mla_w_ukv",
               "s5_lam_re", "s5_lam_im", "s5_b_re", "s5_b_im", "s5_c_re", "s5_c_im", "s5_d", "s5_log_dt",
               "s5_glu_w", "s5_glu_b", "w_out", "norm2_g", "ffn_w_up", "ffn_conv_w", "ffn_w_down")


def kernel(x, norm1_g, w_in, gdn_conv_w, gdn_a_log, gdn_dt_bias, gdn_norm_g, mla_q_norm_g, mla_w_uq,
           mla_kv_norm_g, mla_w_ukv, s5_lam_re, s5_lam_im, s5_b_re, s5_b_im, s5_c_re, s5_c_im, s5_d,
           s5_log_dt, s5_glu_w, s5_glu_b, w_out, norm2_g, ffn_w_up, ffn_conv_w, ffn_w_down, final_norm_g):
    stacked = (norm1_g, w_in, gdn_conv_w, gdn_a_log, gdn_dt_bias, gdn_norm_g, mla_q_norm_g, mla_w_uq,
               mla_kv_norm_g, mla_w_ukv, s5_lam_re, s5_lam_im, s5_b_re, s5_b_im, s5_c_re, s5_c_im, s5_d,
               s5_log_dt, s5_glu_w, s5_glu_b, w_out, norm2_g, ffn_w_up, ffn_conv_w, ffn_w_down)
    b, l, dm = x.shape
    inv_freq = ROPE_THETA ** (-jnp.arange(0, MLA_ROPE, 2, dtype=F32) / MLA_ROPE)
    ang = jnp.arange(l, dtype=F32)[:, None] * inv_freq[None, :]
    cos, sin = jnp.cos(ang), jnp.sin(ang)
    depth = norm1_g.shape[0]
    for i in range(depth):
        x = _layer(x, cos, sin, {name: a[i] for name, a in zip(_LAYER_KEYS, stacked)})
    return rmsnorm_rows(x.reshape(b * l, dm), final_norm_g, 512).reshape(b, l, dm)
```

```python
import functools
import math

import jax
import jax.numpy as jnp
from jax import lax
from jax.experimental import pallas as pl
from jax.experimental.pallas import tpu as pltpu

F32 = jnp.float32
BF16 = jnp.bfloat16
NEG_INF = float("-inf")
EPS = 1e-6
HIGHEST = lax.Precision.HIGHEST

D_MODEL = 1024
GDN_HEADS, GDN_DK, GDN_DV, GDN_CONV, GDN_CHUNK = 4, 64, 64, 4, 64
MOBA_HEADS, MOBA_HD, MOBA_BLOCK, MOBA_TOPK = 4, 64, 256, 3
MLA_HEADS, MLA_Q_LORA, MLA_KV_LORA, MLA_NOPE, MLA_ROPE, MLA_V = 4, 192, 128, 64, 32, 64
ROPE_THETA = 10000.0
S5_GROUP, S5_GROUPS, S5_STATE = 16, 16, 64
S5_WIDTH = S5_GROUP * S5_GROUPS
D_FF, FFN_CONV = 2816, 3

LANES = 128
SUBLANES = 8
VMEM_LIMIT = 56 * 1024 * 1024

COL_GQKV, COL_MQKV, COL_AZ, COL_SU, COL_CQ, COL_CKV, COL_SMA, COL_SMB = (
    0, 768, 1536, 1792, 2048, 2304, 2432, 2560)
N_PROJ = 2688
S5_CHUNK = 16


def _cparams(*sems):
    return pltpu.CompilerParams(dimension_semantics=sems, vmem_limit_bytes=VMEM_LIMIT)


def _dot(a, b):
    return jnp.dot(a.astype(BF16), b.astype(BF16), preferred_element_type=F32)


def _dot_nt(a, b):
    return lax.dot_general(a.astype(BF16), b.astype(BF16), (((1,), (1,)), ((), ())),
                           preferred_element_type=F32)


def _sigmoid(x):
    return jax.nn.sigmoid(x)


def _silu(x):
    return x * jax.nn.sigmoid(x)


def _in_proj_body(x_ref, g_ref, w_ref, o_ref, su_ref, *, inv_dim):
    x = x_ref[...]
    ms = jnp.sum(x * x, axis=-1, keepdims=True) * inv_dim
    h = x * lax.rsqrt(ms + EPS) * g_ref[...]
    proj = _dot(h, w_ref[...])
    o_ref[...] = proj
    su_ref[...] = proj[:, COL_SU:COL_SU + S5_WIDTH].astype(BF16)


def in_proj(x, g, w, tm):
    t, k = x.shape
    n = w.shape[1]
    return pl.pallas_call(
        functools.partial(_in_proj_body, inv_dim=1.0 / k),
        grid=(t // tm,),
        in_specs=[pl.BlockSpec((tm, k), lambda i: (i, 0)),
                  pl.BlockSpec((1, k), lambda i: (0, 0)),
                  pl.BlockSpec((k, n), lambda i: (0, 0))],
        out_specs=[pl.BlockSpec((tm, n), lambda i: (i, 0)), pl.BlockSpec((tm, S5_WIDTH), lambda i: (i, 0))],
        out_shape=[jax.ShapeDtypeStruct((t, n), F32), jax.ShapeDtypeStruct((t, S5_WIDTH), BF16)],
        compiler_params=_cparams("parallel"),
        name="in_proj",
    )(x, g.reshape(1, k), w)


def _rms_body(x_ref, g_ref, o_ref, *, inv_dim):
    x = x_ref[...]
    ms = jnp.sum(x * x, axis=-1, keepdims=True) * inv_dim
    o_ref[...] = x * lax.rsqrt(ms + EPS) * g_ref[...]


def rmsnorm_rows(x, g, tm):
    t, k = x.shape
    return pl.pallas_call(
        functools.partial(_rms_body, inv_dim=1.0 / k),
        grid=(t // tm,),
        in_specs=[pl.BlockSpec((tm, k), lambda i: (i, 0)),
                  pl.BlockSpec((1, k), lambda i: (0, 0))],
        out_specs=pl.BlockSpec((tm, k), lambda i: (i, 0)),
        out_shape=jax.ShapeDtypeStruct((t, k), F32),
        compiler_params=_cparams("parallel"),
        name="final_rmsnorm",
    )(x, g.reshape(1, k))


def _gdn_prep_body(qkv_ref, halo_ref, sm_ref, cw_ref, alog_ref, dtb_ref, tri_ref,
                   q_ref, k_ref, v_ref, beta_ref, gam_ref, gr_ref, xs_ref, *, tm):
    i = pl.program_id(1)
    keep = (i > 0).astype(F32)
    xs_ref[0:SUBLANES, :] = halo_ref[0] * keep
    xs_ref[SUBLANES:, :] = qkv_ref[0]
    w = cw_ref[...]
    y = w[3:4, :] * xs_ref[pl.ds(SUBLANES, tm), :]
    for tap in range(GDN_CONV - 1):
        y = y + w[tap:tap + 1, :] * xs_ref[pl.ds(SUBLANES - (GDN_CONV - 1) + tap, tm), :]
    y = _silu(y)
    sm = sm_ref[0]
    g = -jnp.exp(alog_ref[...]) * jax.nn.softplus(sm + dtb_ref[...])
    beta = _sigmoid(sm)
    gam = jnp.dot(tri_ref[...], g, precision=HIGHEST, preferred_element_type=F32)
    pick = (lax.broadcasted_iota(jnp.int32, (SUBLANES, LANES), 0)
            == lax.broadcasted_iota(jnp.int32, (SUBLANES, LANES), 1)).astype(F32)
    gr_ref[0] = lax.dot_general(pick, gam, (((1,), (1,)), ((), ())), precision=HIGHEST,
                                preferred_element_type=F32)
    nh, dk = GDN_HEADS, GDN_DK
    for h in range(nh):
        qh = y[:, h * dk:(h + 1) * dk]
        kh = y[:, nh * dk + h * dk: nh * dk + (h + 1) * dk]
        vh = y[:, 2 * nh * dk + h * dk: 2 * nh * dk + (h + 1) * dk]
        q_ref[0, h] = qh * lax.rsqrt(jnp.sum(qh * qh, axis=-1, keepdims=True) + EPS) * (dk ** -0.5)
        k_ref[0, h] = kh * lax.rsqrt(jnp.sum(kh * kh, axis=-1, keepdims=True) + EPS)
        v_ref[0, h] = vh
        beta_ref[0, h] = jnp.broadcast_to(beta[:, nh + h: nh + h + 1], (tm, dk))
        gam_ref[0, h] = jnp.broadcast_to(gam[:, h:h + 1], (tm, dk))


def gdn_prep(proj3, conv_w, a_log, dt_bias, tm):
    b, l, _ = proj3.shape
    nh, dk = GDN_HEADS, GDN_DK
    wq = 3 * nh * dk
    alog = jnp.zeros((1, LANES), F32).at[0, :nh].set(a_log)
    dtb = jnp.zeros((1, LANES), F32).at[0, :nh].set(dt_bias)
    r = jnp.arange(tm)
    tri = ((r[:, None] // GDN_CHUNK == r[None, :] // GDN_CHUNK) & (r[None, :] <= r[:, None])).astype(F32)
    hm = jax.ShapeDtypeStruct((b, nh, l, dk), F32)
    hm_spec = pl.BlockSpec((1, nh, tm, dk), lambda bi, i: (bi, 0, i, 0))
    halo_blocks = tm // SUBLANES
    return pl.pallas_call(
        functools.partial(_gdn_prep_body, tm=tm),
        grid=(b, l // tm),
        in_specs=[pl.BlockSpec((1, tm, wq), lambda bi, i: (bi, i, COL_GQKV // wq)),
                  pl.BlockSpec((1, SUBLANES, wq),
                               lambda bi, i: (bi, jnp.maximum(i * halo_blocks - 1, 0), COL_GQKV // wq)),
                  pl.BlockSpec((1, tm, LANES), lambda bi, i: (bi, i, COL_SMA // LANES)),
                  pl.BlockSpec((GDN_CONV, wq), lambda bi, i: (0, 0)),
                  pl.BlockSpec((1, LANES), lambda bi, i: (0, 0)),
                  pl.BlockSpec((1, LANES), lambda bi, i: (0, 0)),
                  pl.BlockSpec((tm, tm), lambda bi, i: (0, 0))],
        out_specs=[hm_spec] * 5 + [pl.BlockSpec((1, SUBLANES, tm), lambda bi, i: (bi, 0, i))],
        out_shape=[hm] * 5 + [jax.ShapeDtypeStruct((b, SUBLANES, l), F32)],
        scratch_shapes=[pltpu.VMEM((tm + SUBLANES, wq), F32)],
        compiler_params=_cparams("parallel", "arbitrary"),
        name="gdn_prep",
    )(proj3, proj3, proj3, conv_w, alog, dtb, tri)


def _split_bf16(a):
    hi = a.astype(BF16)
    lo = (a - hi.astype(F32)).astype(BF16)
    return hi, lo


def _bmm3(a, b):
    ah, al = _split_bf16(a)
    bh, bl = _split_bf16(b)
    f = lambda u, v: jnp.einsum("nij,njk->nik", u, v, preferred_element_type=F32)
    return f(ah, bh) + f(ah, bl) + f(al, bh)


def _gdn_chunk_body(q_ref, k_ref, v_ref, beta_ref, gam_ref, gr_ref, z_ref, ng_ref, o_ref, s_ref,
                    *, nc):
    c = GDN_CHUNK
    nh, dk, dv = GDN_HEADS, GDN_DK, GDN_DV
    assert c == dk == dv

    @pl.when(pl.program_id(1) == 0)
    def _():
        s_ref[...] = jnp.zeros_like(s_ref)

    nb = nh * nc
    rows = lax.broadcasted_iota(jnp.int32, (nb, c, c), 1)
    cols = lax.broadcasted_iota(jnp.int32, (nb, c, c), 2)
    causal = rows >= cols
    strict = rows > cols
    eye = (rows == cols).astype(F32)
    q = q_ref[0].reshape(nb, c, dk)
    k = k_ref[0].reshape(nb, c, dk)
    v = v_ref[0].reshape(nb, c, dv)
    beta = beta_ref[0].reshape(nb, c, c)
    gc = gam_ref[0].reshape(nb, c, c)
    gr = jnp.concatenate([gr_ref[0, h:h + 1, n * c:(n + 1) * c].reshape(1, 1, c)
                          for h in range(nh) for n in range(nc)], axis=0)
    gr = jnp.broadcast_to(gr, (nb, c, c))
    decay = jnp.exp(jnp.where(causal, gc - gr, NEG_INF))
    kb = k.astype(BF16)
    qk = jnp.einsum("nid,njd->nij", jnp.concatenate([q.astype(BF16), kb], axis=1), kb,
                    preferred_element_type=F32)
    attn = qk[:, :c] * decay
    x = -jnp.where(strict, beta * qk[:, c:] * decay, 0.0)
    p = eye + x
    for _ in range(5):
        x = _bmm3(x, x)
        p = p + _bmm3(p, x)
    eg = jnp.exp(gc)
    uw = _bmm3(p, jnp.concatenate([beta * v, beta * eg * k], axis=2))
    g_last = gc[:, c - 1:c, :]
    split = lambda a: a.reshape((nh, nc) + a.shape[1:])
    u, w = split(uw[:, :, :dv]), split(uw[:, :, dv:])
    attn, q_dec = split(attn), split(q * eg)
    k_dec = split(k * jnp.exp(g_last - gc))
    gl = split(jnp.exp(g_last))
    ng = ng_ref[...]
    s = s_ref[...]
    for n in range(nc):
        sb = s.astype(BF16)
        r = jnp.einsum("hik,hkv->hiv", jnp.concatenate([w[:, n], q_dec[:, n]], axis=1).astype(BF16), sb,
                       preferred_element_type=F32)
        v_new = u[:, n] - r[:, :c]
        vb = v_new.astype(BF16)
        o = r[:, c:] + jnp.einsum("hij,hjv->hiv", attn[:, n].astype(BF16), vb, preferred_element_type=F32)
        upd = [lax.dot_general(k_dec[h, n].astype(BF16), vb[h], (((0,), (0,)), ((), ())),
                               preferred_element_type=F32) for h in range(nh)]
        s = gl[:, n] * s + jnp.stack(upd, axis=0)
        o = o * lax.rsqrt(jnp.mean(o * o, axis=-1, keepdims=True) + EPS) * ng
        o_all = jnp.concatenate([o[h] for h in range(nh)], axis=1)
        o_ref[0, n * c:(n + 1) * c, :] = (o_all * _silu(z_ref[0, n * c:(n + 1) * c, :])).astype(o_ref.dtype)
    s_ref[...] = s


def gdn_chunk(q, k, v, beta, gam, gr, proj3, norm_g, lt):
    b, nh, l, dk = q.shape
    nc = lt // GDN_CHUNK
    wz = nh * GDN_DV
    hm_spec = pl.BlockSpec((1, nh, lt, dk), lambda bi, i: (bi, 0, i, 0))
    return pl.pallas_call(
        functools.partial(_gdn_chunk_body, nc=nc),
        grid=(b, l // lt),
        in_specs=[hm_spec] * 5 + [
            pl.BlockSpec((1, SUBLANES, lt), lambda bi, i: (bi, 0, i)),
            pl.BlockSpec((1, lt, wz), lambda bi, i: (bi, i, COL_AZ // wz)),
            pl.BlockSpec((1, GDN_DV), lambda bi, i: (0, 0))],
        out_specs=pl.BlockSpec((1, lt, wz), lambda bi, i: (bi, i, 0)),
        out_shape=jax.ShapeDtypeStruct((b, l, wz), BF16),
        scratch_shapes=[pltpu.VMEM((nh, GDN_DK, GDN_DV), F32)],
        compiler_params=_cparams("parallel", "arbitrary"),
        name="gdn_chunk",
    )(q, k, v, beta, gam, gr, proj3, norm_g.reshape(1, GDN_DV))


MASK_BIAS = -1e30
LOG2E = 1.4426950408889634


def _moba_gate_body(q_ref, kall_ref, k_ref, v_ref, pq_ref, pb_ref, qa_ref, ka_ref, vb_ref, km_ref, *, nb):
    i = pl.program_id(1)
    nh, hd = MOBA_HEADS, MOBA_HD
    w = nh * hd

    assert nb & (nb - 1) == 0 and hd & (hd - 1) == 0
    nb_bits, hd_bits = nb.bit_length() - 1, hd.bit_length() - 1

    @pl.when(i == 0)
    def _():
        kmean = jnp.mean(kall_ref[0].reshape(nb, MOBA_BLOCK, w), axis=1)
        lane = lax.broadcasted_iota(jnp.int32, (nb, w), 1)
        for h in range(nh):
            km_ref[h * nb:(h + 1) * nb, :] = jnp.where(jnp.right_shift(lane, hd_bits) == h, kmean, 0.0)

    q = q_ref[0]
    gate = lax.dot_general(km_ref[...], q, (((1,), (1,)), ((), ())),
                           precision=HIGHEST, preferred_element_type=F32)
    blk = lax.broadcasted_iota(jnp.int32, (nb, gate.shape[1]), 0)
    valid = blk < i
    big = jnp.int32(nb)
    masked = []
    for h in range(nh):
        g_h = gate[h * nb:(h + 1) * nb, :]
        avail = valid
        for _ in range(MOBA_TOPK):
            m = jnp.max(jnp.where(avail, g_h, NEG_INF), axis=0, keepdims=True)
            cand = avail & (g_h == m)
            first = jnp.min(jnp.where(cand, blk, big), axis=0, keepdims=True)
            avail = avail & (blk != first)
        masked.append(avail.astype(F32))
    masked = jnp.concatenate(masked, axis=0).astype(BF16)
    bias = lax.dot_general(masked, pb_ref[...], (((0,), (0,)), ((), ())), preferred_element_type=F32)
    q_tiles = _dot(q * (hd ** -0.5 * LOG2E), pq_ref[...]) + bias * MASK_BIAS
    qa_ref[0] = q_tiles.astype(BF16)
    tile_lane = jnp.bitwise_and(lax.broadcasted_iota(jnp.int32, q_tiles.shape, 1), LANES - 1)
    ka_ref[0] = (_dot(k_ref[0], pq_ref[...]) + (tile_lane == hd + i).astype(F32)).astype(BF16)
    vb_ref[0] = v_ref[0].astype(BF16)


def moba_gate(proj3):
    b, l, _ = proj3.shape
    nh, hd = MOBA_HEADS, MOBA_HD
    w = nh * hd
    nb = l // MOBA_BLOCK
    tq = MOBA_BLOCK
    assert hd + nb <= LANES
    src = jnp.arange(w)
    place_q = jnp.zeros((w, nh * LANES), F32).at[src, (src // hd) * LANES + src % hd].set(1.0).astype(BF16)
    srcb = jnp.arange(nh * nb)
    place_b = jnp.zeros((nh * nb, nh * LANES), F32).at[srcb, (srcb // nb) * LANES + hd + srcb % nb].set(1.0).astype(BF16)
    cq = COL_MQKV // w
    row = lambda c: pl.BlockSpec((1, tq, w), lambda bi, i: (bi, i, c))
    full = lambda a: pl.BlockSpec(a.shape, lambda bi, i: (0, 0))
    tiles = jax.ShapeDtypeStruct((b, l, nh * LANES), BF16)
    tspec = pl.BlockSpec((1, tq, nh * LANES), lambda bi, i: (bi, i, 0))
    return pl.pallas_call(
        functools.partial(_moba_gate_body, nb=nb),
        grid=(b, l // tq),
        in_specs=[row(cq), pl.BlockSpec((1, l, w), lambda bi, i: (bi, 0, cq + 1)), row(cq + 1), row(cq + 2),
                  full(place_q), full(place_b)],
        out_specs=[tspec, tspec, pl.BlockSpec((1, tq, w), lambda bi, i: (bi, i, 0))],
        out_shape=[tiles, tiles, jax.ShapeDtypeStruct((b, l, w), BF16)],
        scratch_shapes=[pltpu.VMEM((nh * nb, w), F32)],
        compiler_params=_cparams("parallel", "arbitrary"),
        name="moba_gate",
    )(proj3, proj3, proj3, proj3, place_q, place_b)


def _flash_body(q_ref, k_ref, v_ref, o_ref, m_scr, acc_scr, *, nh, tq):
    i = pl.program_id(1)
    row = lax.broadcasted_iota(jnp.int32, (tq, tq), 0)
    col = lax.broadcasted_iota(jnp.int32, (tq, tq), 1)
    tri = col <= row
    ones = jnp.ones((tq, LANES), BF16)

    def tiles(kv_r0, first):
        for h in range(nh):
            q = q_ref[0, :, h * LANES:(h + 1) * LANES]
            kt = k_ref[0, pl.ds(kv_r0, tq), h * LANES:(h + 1) * LANES]
            v_lo = (h // 2) * LANES
            vt = jnp.concatenate([v_ref[0, pl.ds(kv_r0, tq), v_lo:v_lo + LANES], ones], axis=1)
            s = _dot_nt(q, kt)
            if first:
                s = jnp.where(tri, s, NEG_INF)
            m_cur = jnp.max(s, axis=1, keepdims=True)
            m_next = jnp.broadcast_to(m_cur, (tq, LANES)) if first else jnp.maximum(m_scr[h], m_cur)
            p = jnp.exp2(s - jnp.concatenate([m_next] * (tq // LANES), axis=1))
            pv = _dot(p, vt)
            if first:
                acc_scr[h] = pv
            else:
                alpha = jnp.exp2(m_scr[h] - m_next)
                acc_scr[h] = jnp.concatenate([alpha, alpha], axis=1) * acc_scr[h] + pv
            m_scr[h] = m_next

    tiles(pl.multiple_of(i * tq, tq), True)

    def past(j, carry):
        tiles(pl.multiple_of(j * tq, tq), False)
        return carry

    lax.fori_loop(0, i, past, 0)
    lane = lax.broadcasted_iota(jnp.int32, (tq, LANES), 1)
    low = lane < LANES // 2
    outs = [acc_scr[h][:, :LANES] / acc_scr[h][:, LANES:] for h in range(nh)]
    pairs = [jnp.where(low, outs[2 * p], outs[2 * p + 1]) for p in range(nh // 2)]
    o_ref[0] = jnp.concatenate(pairs, axis=1).astype(o_ref.dtype)


def flash_attention(q, k, v, *, nh, name):
    b, l, wq = q.shape
    wv = v.shape[2]
    tq = MOBA_BLOCK
    return pl.pallas_call(
        functools.partial(_flash_body, nh=nh, tq=tq),
        grid=(b, l // tq),
        in_specs=[pl.BlockSpec((1, tq, wq), lambda bi, i: (bi, i, 0)),
                  pl.BlockSpec((1, l, wq), lambda bi, i: (bi, 0, 0)),
                  pl.BlockSpec((1, l, wv), lambda bi, i: (bi, 0, 0))],
        out_specs=pl.BlockSpec((1, tq, wv), lambda bi, i: (bi, i, 0)),
        out_shape=jax.ShapeDtypeStruct((b, l, wv), BF16),
        scratch_shapes=[pltpu.VMEM((nh, tq, LANES), F32), pltpu.VMEM((nh, tq, 2 * LANES), F32)],
        compiler_params=_cparams("parallel", "arbitrary"),
        name=name,
    )(q, k, v)


def _mla_proj_body(cq_ref, ckv_ref, sma_ref, smb_ref, qg_ref, kvg_ref, wq_ref, wkv_ref,
                   cq_t, sq_t, ck_t, sk_t, q_ref, k_ref, v_ref):
    nh = MLA_HEADS
    cq = cq_ref[0]
    hq = cq * lax.rsqrt(jnp.sum(cq * cq, axis=-1, keepdims=True) * (1.0 / MLA_Q_LORA) + EPS) * qg_ref[...]
    qq = _dot(hq, wq_ref[...])
    half = nh * LANES
    q_ref[0] = (qq[:, :half] * cq_t[...] + qq[:, half:] * sq_t[...]).astype(BF16)
    ckv = ckv_ref[0]
    hkv = ckv * lax.rsqrt(jnp.mean(ckv * ckv, axis=-1, keepdims=True) + EPS) * kvg_ref[...]
    kv = _dot(hkv, wkv_ref[...])
    krot = sma_ref[0] * ck_t[...] + smb_ref[0] * sk_t[...]
    k_ref[0] = (kv[:, :half] + jnp.concatenate([krot] * nh, axis=1)).astype(BF16)
    v_ref[0] = kv[:, half:].astype(BF16)


def mla_proj(proj3, q_norm_g, w_uq, kv_norm_g, w_ukv, cos, sin, tm):
    b, l, _ = proj3.shape
    nh, dn, dr, dvh = MLA_HEADS, MLA_NOPE, MLA_ROPE, MLA_V
    hr = dr // 2
    scale = (dn + dr) ** -0.5 * LOG2E
    cqw = COL_CKV - COL_CQ
    w1 = jnp.zeros((cqw, nh, LANES), F32)
    w2 = jnp.zeros((cqw, nh, LANES), F32)
    wq3 = w_uq.reshape(MLA_Q_LORA, nh, dn + dr)
    w1 = w1.at[:MLA_Q_LORA, :, :dn + dr].set(wq3)
    w2 = w2.at[:MLA_Q_LORA, :, dn:dn + hr].set(-wq3[:, :, dn + hr:])
    w2 = w2.at[:MLA_Q_LORA, :, dn + hr:dn + dr].set(wq3[:, :, dn:dn + hr])
    wq = jnp.concatenate([w1.reshape(cqw, nh * LANES), w2.reshape(cqw, nh * LANES)], axis=1).astype(BF16)
    qg = jnp.zeros((1, cqw), F32).at[0, :MLA_Q_LORA].set(q_norm_g)
    wkv3 = w_ukv.reshape(MLA_KV_LORA, nh, dn + dvh)
    wk = jnp.zeros((MLA_KV_LORA, nh, LANES), F32).at[:, :, :dn].set(wkv3[:, :, :dn])
    wkv = jnp.concatenate([wk.reshape(MLA_KV_LORA, nh * LANES),
                           wkv3[:, :, dn:].reshape(MLA_KV_LORA, nh * dvh)], axis=1).astype(BF16)
    cc = jnp.concatenate([cos, cos], axis=1)
    ss = jnp.concatenate([sin, sin], axis=1)
    pad = jnp.zeros((l, LANES - dn - dr), F32)
    cq_tile = jnp.concatenate([jnp.ones((l, dn), F32), cc, pad], axis=1) * scale
    sq_tile = jnp.concatenate([jnp.zeros((l, dn), F32), ss, pad], axis=1) * scale
    cq_t = jnp.tile(cq_tile, (1, nh))
    sq_t = jnp.tile(sq_tile, (1, nh))
    ck_t = jnp.concatenate([jnp.zeros((l, dn), F32), cc, pad], axis=1)
    sk_t = jnp.concatenate([jnp.zeros((l, dn), F32), ss, pad], axis=1)
    row = lambda w_, c_: pl.BlockSpec((1, tm, w_), lambda bi, i: (bi, i, c_ // w_))
    full = lambda a: pl.BlockSpec(a.shape, lambda bi, i: (0, 0))
    tab = lambda w_: pl.BlockSpec((tm, w_), lambda bi, i: (i, 0))
    out_q = jax.ShapeDtypeStruct((b, l, nh * LANES), BF16)
    out_v = jax.ShapeDtypeStruct((b, l, nh * dvh), BF16)
    kvg = kv_norm_g.reshape(1, MLA_KV_LORA)
    return pl.pallas_call(
        _mla_proj_body,
        grid=(b, l // tm),
        in_specs=[row(cqw, COL_CQ), row(MLA_KV_LORA, COL_CKV), row(LANES, COL_SMA), row(LANES, COL_SMB),
                  full(qg), full(kvg), full(wq), full(wkv),
                  tab(nh * LANES), tab(nh * LANES), tab(LANES), tab(LANES)],
        out_specs=[pl.BlockSpec((1, tm, nh * LANES), lambda bi, i: (bi, i, 0)),
                   pl.BlockSpec((1, tm, nh * LANES), lambda bi, i: (bi, i, 0)),
                   pl.BlockSpec((1, tm, nh * dvh), lambda bi, i: (bi, i, 0))],
        out_shape=[out_q, out_q, out_v],
        compiler_params=_cparams("parallel", "parallel"),
        name="mla_proj",
    )(proj3, proj3, proj3, proj3, qg, kvg, wq, wkv, cq_t, sq_t, ck_t, sk_t)


def _s5_state_body(u_ref, pst_ref, lam_ref, xp_ref, pu_scr, xp_scr, *, nch, nseq):
    half = S5_GROUPS * S5_STATE
    pu_scr[...] = _dot(u_ref[...], pst_ref[...])
    ar = lam_ref[0:1, :]
    ai = lam_ref[1:2, :]

    def step(kc, carry):
        nxt = []
        for sq in range(nseq):
            xr, xi = carry[sq]
            r = sq * nch + kc
            xp_scr[pl.ds(r, 1), :] = jnp.concatenate([xr, xi], axis=1)
            e = pu_scr[pl.ds(r, 1), :]
            nxt.append((ar * xr - ai * xi + e[:, :half], ar * xi + ai * xr + e[:, half:]))
        return tuple(nxt)

    zero = jnp.zeros((1, half), F32)
    lax.fori_loop(0, nch, step, tuple((zero, zero) for _ in range(nseq)))
    xp_ref[...] = xp_scr[...].astype(BF16)


def _s5_out_body(u_ref, xp_ref, kr_ref, qst_ref, y_ref):
    cn, w = S5_CHUNK, S5_WIDTH
    xp = xp_ref[...]
    for t in range(cn):
        y_ref[:, t * w:(t + 1) * w] = (_dot(u_ref[:, :(t + 1) * w], kr_ref[(cn - 1 - t) * w:, :])
                                       + _dot(xp, qst_ref[:, t * w:(t + 1) * w]))


def s5_weights(lam_re, lam_im, b_re, b_im, c_re, c_im, d, log_dt):
    cn = S5_CHUNK
    g, p, cg = S5_GROUPS, S5_STATE, S5_GROUP
    ein = functools.partial(jnp.einsum, precision=HIGHEST)
    eye = jnp.eye(g, dtype=F32)
    lr = jnp.minimum(lam_re, -1e-4)
    li = lam_im
    dt = jnp.exp(log_dt)[:, None]
    mag = jnp.exp(lr * dt)
    ar, ai = mag * jnp.cos(li * dt), mag * jnp.sin(li * dt)
    den = lr * lr + li * li
    cr = ((ar - 1.0) * lr + ai * li) / den
    ci = (ai * lr - (ar - 1.0) * li) / den
    bbr = cr[..., None] * b_re - ci[..., None] * b_im
    bbi = cr[..., None] * b_im + ci[..., None] * b_re
    tau = jnp.arange(cn + 1, dtype=F32)[None, :, None]
    pmag = jnp.exp((lr * dt)[:, None, :] * tau)
    pang = (li * dt)[:, None, :] * tau
    pwr, pwi = pmag * jnp.cos(pang), pmag * jnp.sin(pang)
    clr = c_re[:, None] * pwr[:, :, None, :] - c_im[:, None] * pwi[:, :, None, :]
    cli = c_re[:, None] * pwi[:, :, None, :] + c_im[:, None] * pwr[:, :, None, :]
    kt = ein("gtap,gpc->gtac", clr[:, :cn], bbr) - ein("gtap,gpc->gtac", cli[:, :cn], bbi)
    kt = kt.at[:, 0].add(d[:, :, None] * jnp.eye(cg, dtype=F32)[None])
    kbd = jnp.einsum("gtac,gh->tgcha", kt, eye).reshape(cn, g * cg, g * cg)
    kr = kbd[::-1].reshape(cn * g * cg, g * cg)
    rev_r, rev_i = pwr[:, cn - 1::-1][:, :cn], pwi[:, cn - 1::-1][:, :cn]
    bbr_t, bbi_t = bbr.transpose(0, 2, 1)[:, None], bbi.transpose(0, 2, 1)[:, None]
    pr = rev_r[:, :, None, :] * bbr_t - rev_i[:, :, None, :] * bbi_t
    pi = rev_r[:, :, None, :] * bbi_t + rev_i[:, :, None, :] * bbr_t
    pst = jnp.stack([jnp.einsum("gscp,gh->sgchp", m, eye) for m in (pr, pi)], axis=3)
    pst = pst.reshape(cn * g * cg, 2 * g * p)
    qst = jnp.stack([jnp.einsum("gtap,gh->gptha", m, eye) for m in (clr[:, 1:], -cli[:, 1:])], axis=0)
    qst = qst.reshape(2 * g * p, cn * g * cg)
    lam = jnp.stack([pwr[:, cn].reshape(g * p), pwi[:, cn].reshape(g * p)], axis=0)
    return kr.astype(BF16), pst.astype(BF16), qst.astype(BF16), lam


def s5_scan(su, weights, batch):
    t, w = su.shape
    cn = S5_CHUNK
    kr, pst, qst, lam = weights
    nr = t // cn
    nch = nr // batch
    nseq = 2
    rows = nseq * nch
    u = su.reshape(nr, cn * w)
    ns = pst.shape[1]
    once = lambda a: pl.BlockSpec(a.shape, lambda i: (0, 0), pipeline_mode=pl.Buffered(1))
    rowb = lambda width: pl.BlockSpec((rows, width), lambda i: (i, 0))
    xp = pl.pallas_call(
        functools.partial(_s5_state_body, nch=nch, nseq=nseq),
        grid=(nr // rows,),
        in_specs=[rowb(cn * w), once(pst), once(lam)],
        out_specs=rowb(ns),
        out_shape=jax.ShapeDtypeStruct((nr, ns), BF16),
        scratch_shapes=[pltpu.VMEM((rows, ns), F32)] * 2,
        compiler_params=_cparams("parallel"),
        name="s5_state",
    )(u, pst, lam)
    y = pl.pallas_call(
        _s5_out_body,
        grid=(nr // rows,),
        in_specs=[rowb(cn * w), rowb(ns), once(kr), once(qst)],
        out_specs=rowb(cn * w),
        out_shape=jax.ShapeDtypeStruct((nr, cn * w), F32),
        compiler_params=_cparams("parallel"),
        name="s5_out",
    )(u, xp, kr, qst)
    return y.reshape(t, w)


def _out_proj_body(x_ref, oa_ref, ob_ref, oc_ref, ys_ref, gw_ref, gb_ref, wo_ref, o_ref):
    y = jax.nn.gelu(ys_ref[...])
    od = y * _sigmoid(_dot(y, gw_ref[...]) + gb_ref[...])
    mix = jnp.concatenate([oa_ref[...].astype(BF16), ob_ref[...], oc_ref[...], od.astype(BF16)], axis=1)
    o_ref[...] = x_ref[...] + _dot(mix, wo_ref[...])


def out_proj(x, oa, ob, oc, ys, glu_w, glu_b, w_out, tm):
    t, dm = x.shape
    wm = oa.shape[1]
    row = lambda w_: pl.BlockSpec((tm, w_), lambda i: (i, 0))
    full = lambda a: pl.BlockSpec(a.shape, lambda i: (0, 0))
    gb = glu_b.reshape(1, wm)
    return pl.pallas_call(
        _out_proj_body,
        grid=(t // tm,),
        in_specs=[row(dm), row(wm), row(wm), row(wm), row(wm), full(glu_w), full(gb), full(w_out)],
        out_specs=row(dm),
        out_shape=jax.ShapeDtypeStruct((t, dm), F32),
        compiler_params=_cparams("parallel"),
        name="out_proj",
    )(x, oa, ob, oc, ys, glu_w, gb, w_out)


def _ffn_body(x_ref, halo_ref, g_ref, wup_ref, cw_ref, wdn_ref, o_ref, up_scr, *, tm, fc):
    i = pl.program_id(1)
    keep = (i > 0).astype(F32)
    g = g_ref[...]

    def norm(v):
        return (v * lax.rsqrt(jnp.mean(v * v, axis=-1, keepdims=True) + EPS) * g).astype(BF16)

    x = x_ref[0]
    h_main = norm(x)
    h_halo = norm(halo_ref[0])
    acc = x
    for c in range(D_FF // fc):
        acts = []
        for part in range(2):
            lo = part * D_FF + c * fc
            up_scr[0:SUBLANES, :] = _dot(h_halo, wup_ref[:, lo:lo + fc]) * keep
            up_scr[SUBLANES:, :] = _dot(h_main, wup_ref[:, lo:lo + fc])
            w = cw_ref[:, lo:lo + fc]
            y = w[FFN_CONV - 1:FFN_CONV, :] * up_scr[pl.ds(SUBLANES, tm), :]
            for tap in range(FFN_CONV - 1):
                y = y + w[tap:tap + 1, :] * up_scr[pl.ds(SUBLANES - (FFN_CONV - 1) + tap, tm), :]
            acts.append(y)
        act = _silu(acts[0]) * acts[1]
        acc = acc + _dot(act, wdn_ref[c * fc:(c + 1) * fc, :])
    o_ref[0] = acc


def ffn(x3, norm_g, w_up, conv_w, w_down, tm, fc):
    b, l, dm = x3.shape
    halo_blocks = tm // SUBLANES
    once = lambda a: pl.BlockSpec(a.shape, lambda bi, i: (0, 0), pipeline_mode=pl.Buffered(1))
    g = norm_g.reshape(1, dm)
    return pl.pallas_call(
        functools.partial(_ffn_body, tm=tm, fc=fc),
        grid=(b, l // tm),
        in_specs=[pl.BlockSpec((1, tm, dm), lambda bi, i: (bi, i, 0)),
                  pl.BlockSpec((1, SUBLANES, dm), lambda bi, i: (bi, jnp.maximum(i * halo_blocks - 1, 0), 0)),
                  once(g), once(w_up), once(conv_w), once(w_down)],
        out_specs=pl.BlockSpec((1, tm, dm), lambda bi, i: (bi, i, 0)),
        out_shape=jax.ShapeDtypeStruct((b, l, dm), F32),
        scratch_shapes=[pltpu.VMEM((tm + SUBLANES, fc), F32)],
        compiler_params=_cparams("parallel", "arbitrary"),
        name="ffn",
    )(x3, x3, g, w_up, conv_w, w_down)


def _permute_w_in(w_in):
    dm = w_in.shape[0]
    gw, mw = GDN_HEADS * GDN_DK, MOBA_HEADS * MOBA_HD
    o_aq, o_az, o_aa, o_ab = 0, 3 * gw, 4 * gw, 4 * gw + GDN_HEADS
    o_mq = o_ab + GDN_HEADS
    o_cq = o_mq + 3 * mw
    o_ckv = o_cq + MLA_Q_LORA
    o_ckr = o_ckv + MLA_KV_LORA
    o_su = o_ckr + MLA_ROPE
    hr = MLA_ROPE // 2
    z = lambda n: jnp.zeros((dm, n), w_in.dtype)
    ckr = w_in[:, o_ckr:o_ckr + MLA_ROPE]
    ckr_swap = jnp.concatenate([-ckr[:, hr:], ckr[:, :hr]], axis=1)
    cols = [w_in[:, o_aq:o_aq + 3 * gw],
            w_in[:, o_mq:o_mq + 3 * mw],
            w_in[:, o_az:o_az + gw],
            w_in[:, o_su:o_su + S5_WIDTH],
            w_in[:, o_cq:o_cq + MLA_Q_LORA], z(COL_CKV - COL_CQ - MLA_Q_LORA),
            w_in[:, o_ckv:o_ckv + MLA_KV_LORA],
            w_in[:, o_aa:o_aa + 2 * GDN_HEADS], z(MLA_NOPE - 2 * GDN_HEADS), ckr, z(LANES - MLA_NOPE - MLA_ROPE),
            z(MLA_NOPE), ckr_swap, z(LANES - MLA_NOPE - MLA_ROPE)]
    return jnp.concatenate(cols, axis=1).astype(BF16)


def _layer(x3, cos, sin, p):
    b, l, dm = x3.shape
    t = b * l
    x2 = x3.reshape(t, dm)
    proj, su = in_proj(x2, p["norm1_g"], _permute_w_in(p["w_in"]), 512)
    proj3 = proj.reshape(b, l, N_PROJ)

    q, k, v, beta, gam, gr = gdn_prep(proj3, p["gdn_conv_w"], p["gdn_a_log"], p["gdn_dt_bias"], 512)
    o_a = gdn_chunk(q, k, v, beta, gam, gr, proj3, p["gdn_norm_g"], 512).reshape(t, GDN_HEADS * GDN_DV)

    mw = MOBA_HEADS * MOBA_HD
    mq, mk, mv = moba_gate(proj3)
    o_b = flash_attention(mq, mk, mv, nh=MOBA_HEADS, name="flash_moba").reshape(t, mw)

    cq, ck, cv = mla_proj(proj3, p["mla_q_norm_g"], p["mla_w_uq"], p["mla_kv_norm_g"], p["mla_w_ukv"],
                          cos, sin, 512)
    o_c = flash_attention(cq, ck, cv, nh=MLA_HEADS, name="flash_mla").reshape(t, MLA_HEADS * MLA_V)

    s5w = s5_weights(p["s5_lam_re"], p["s5_lam_im"], p["s5_b_re"], p["s5_b_im"], p["s5_c_re"], p["s5_c_im"],
                     p["s5_d"], p["s5_log_dt"])
    ys = s5_scan(su, s5w, b)

    x2 = out_proj(x2, o_a, o_b, o_c, ys, p["s5_glu_w"].astype(BF16), p["s5_glu_b"], p["w_out"].astype(BF16), 512)
    return ffn(x2.reshape(b, l, dm), p["norm2_g"], p["ffn_w_up"].astype(BF16), p["ffn_conv_w"],
               p["ffn_w_down"].astype(BF16), 512, D_FF // 2)


_LAYER_KEYS = ("norm1_g", "w_in", "gdn_conv_w", "gdn_a_log", "gdn_dt_bias", "gdn_norm_g",
               "mla_q_norm_g", "mla_w_uq", "mla_kv_norm_g", "mla_w_ukv",
               "s5_lam_re", "s5_lam_im", "s5_b_re", "s5_b_im", "s5_c_re", "s5_c_im", "s5_d", "s5_log_dt",
               "s5_glu_w", "s5_glu_b", "w_out", "norm2_g", "ffn_w_up", "ffn_conv_w", "ffn_w_down")


def kernel(x, norm1_g, w_in, gdn_conv_w, gdn_a_log, gdn_dt_bias, gdn_norm_g, mla_q_norm_g, mla_w_uq,
           mla_kv_norm_g, mla_w_ukv, s5_lam_re, s5_lam_im, s5_b_re, s5_b_im, s5_c_re, s5_c_im, s5_d,
           s5_log_dt, s5_glu_w, s5_glu_b, w_out, norm2_g, ffn_w_up, ffn_conv_w, ffn_w_down, final_norm_g):
    stacked = (norm1_g, w_in, gdn_conv_w, gdn_a_log, gdn_dt_bias, gdn_norm_g, mla_q_norm_g, mla_w_uq,
               mla_kv_norm_g, mla_w_ukv, s5_lam_re, s5_lam_im, s5_b_re, s5_b_im, s5_c_re, s5_c_im, s5_d,
               s5_log_dt, s5_glu_w, s5_glu_b, w_out, norm2_g, ffn_w_up, ffn_conv_w, ffn_w_down)
    b, l, dm = x.shape
    inv_freq = ROPE_THETA ** (-jnp.arange(0, MLA_ROPE, 2, dtype=F32) / MLA_ROPE)
    ang = jnp.arange(l, dtype=F32)[:, None] * inv_freq[None, :]
    cos, sin = jnp.cos(ang), jnp.sin(ang)
    depth = norm1_g.shape[0]
    for i in range(depth):
        x = _layer(x, cos, sin, {name: a[i] for name, a in zip(_LAYER_KEYS, stacked)})
    return rmsnorm_rows(x.reshape(b * l, dm), final_norm_g, 512).reshape(b, l, dm)
```

```python
import functools
import math

import jax
import jax.numpy as jnp
import numpy as np
from jax import lax
from jax.experimental import pallas as pl
from jax.experimental.pallas import tpu as pltpu

F32 = jnp.float32
BF16 = jnp.bfloat16
NEG_INF = float("-inf")
EPS = 1e-6
HIGHEST = lax.Precision.HIGHEST

D_MODEL = 1024
GDN_HEADS, GDN_DK, GDN_DV, GDN_CONV, GDN_CHUNK = 4, 64, 64, 4, 64
MOBA_HEADS, MOBA_HD, MOBA_BLOCK, MOBA_TOPK = 4, 64, 256, 3
MLA_HEADS, MLA_Q_LORA, MLA_KV_LORA, MLA_NOPE, MLA_ROPE, MLA_V = 4, 192, 128, 64, 32, 64
ROPE_THETA = 10000.0
S5_GROUP, S5_GROUPS, S5_STATE = 16, 16, 64
S5_WIDTH = S5_GROUP * S5_GROUPS
D_FF, FFN_CONV = 2816, 3

LANES = 128
SUBLANES = 8
VMEM_LIMIT = 56 * 1024 * 1024

COL_GQKV, COL_MQKV, COL_AZ, COL_SU, COL_CQ, COL_CKV, COL_SMA, COL_SMB = (
    0, 768, 1536, 1792, 2048, 2304, 2432, 2560)
N_PROJ = 2688
S5_CHUNK = 16
ROW_TILE = 512
FFN_ROW_TILE = 1024


def _cparams(*sems):
    return pltpu.CompilerParams(dimension_semantics=sems, vmem_limit_bytes=VMEM_LIMIT)


def _dot(a, b):
    return jnp.dot(a.astype(BF16), b.astype(BF16), preferred_element_type=F32)


def _dot_nt(a, b):
    return lax.dot_general(a.astype(BF16), b.astype(BF16), (((1,), (1,)), ((), ())),
                           preferred_element_type=F32)


def _split3_bf16(x):
    hi = x.astype(BF16)
    r1 = x - hi.astype(F32)
    mid = r1.astype(BF16)
    return hi, mid, (r1 - mid.astype(F32)).astype(BF16)


def _sigmoid(x):
    return jax.nn.sigmoid(x)


def _silu(x):
    return x * jax.nn.sigmoid(x)


def _in_proj_body(x_ref, g_ref, w_ref, o_ref, su_ref, *, inv_dim):
    x = x_ref[...]
    ms = jnp.sum(x * x, axis=-1, keepdims=True) * inv_dim
    h = x * lax.rsqrt(ms + EPS) * g_ref[...]
    proj = _dot(h, w_ref[...])
    o_ref[...] = proj
    su_ref[...] = proj[:, COL_SU:COL_SU + S5_WIDTH].astype(BF16)


def in_proj(x, g, w, tm):
    t, k = x.shape
    n = w.shape[1]
    return pl.pallas_call(
        functools.partial(_in_proj_body, inv_dim=1.0 / k),
        grid=(t // tm,),
        in_specs=[pl.BlockSpec((tm, k), lambda i: (i, 0)),
                  pl.BlockSpec((1, k), lambda i: (0, 0)),
                  pl.BlockSpec((k, n), lambda i: (0, 0))],
        out_specs=[pl.BlockSpec((tm, n), lambda i: (i, 0)), pl.BlockSpec((tm, S5_WIDTH), lambda i: (i, 0))],
        out_shape=[jax.ShapeDtypeStruct((t, n), F32), jax.ShapeDtypeStruct((t, S5_WIDTH), BF16)],
        compiler_params=_cparams("parallel"),
        name="in_proj",
    )(x, g.reshape(1, k), w)


def _gdn_prep_body(qkv_ref, halo_ref, sm_ref, cw_ref, alog_ref, dtb_ref, tri_ref,
                   q_ref, k_ref, v_ref, beta_ref, gam_ref, gr_ref, xs_ref, *, tm):
    i = pl.program_id(1)
    keep = (i > 0).astype(F32)
    xs_ref[0:SUBLANES, :] = halo_ref[0] * keep
    xs_ref[SUBLANES:, :] = qkv_ref[0]
    w = cw_ref[...]
    y = w[3:4, :] * xs_ref[pl.ds(SUBLANES, tm), :]
    for tap in range(GDN_CONV - 1):
        y = y + w[tap:tap + 1, :] * xs_ref[pl.ds(SUBLANES - (GDN_CONV - 1) + tap, tm), :]
    y = _silu(y)
    sm = sm_ref[0]
    g = -jnp.exp(alog_ref[...]) * jax.nn.softplus(sm + dtb_ref[...])
    beta = _sigmoid(sm)
    tri = tri_ref[...]
    gam = sum(jnp.dot(tri, part, preferred_element_type=F32) for part in _split3_bf16(g))
    pick = (lax.broadcasted_iota(jnp.int32, (SUBLANES, LANES), 0)
            == lax.broadcasted_iota(jnp.int32, (SUBLANES, LANES), 1)).astype(BF16)
    gr_ref[0] = sum(lax.dot_general(pick, part, (((1,), (1,)), ((), ())), preferred_element_type=F32)
                    for part in _split3_bf16(gam))
    nh, dk = GDN_HEADS, GDN_DK
    for h in range(nh):
        qh = y[:, h * dk:(h + 1) * dk]
        kh = y[:, nh * dk + h * dk: nh * dk + (h + 1) * dk]
        vh = y[:, 2 * nh * dk + h * dk: 2 * nh * dk + (h + 1) * dk]
        q_ref[0, h] = qh * lax.rsqrt(jnp.sum(qh * qh, axis=-1, keepdims=True) + EPS) * (dk ** -0.5)
        k_ref[0, h] = kh * lax.rsqrt(jnp.sum(kh * kh, axis=-1, keepdims=True) + EPS)
        v_ref[0, h] = vh
        beta_ref[0, h] = jnp.broadcast_to(beta[:, nh + h: nh + h + 1], (tm, dk))
        gam_ref[0, h] = jnp.broadcast_to(gam[:, h:h + 1], (tm, dk))


def gdn_prep(proj3, conv_w, alog, dtb, tm):
    b, l, _ = proj3.shape
    nh, dk = GDN_HEADS, GDN_DK
    wq = 3 * nh * dk
    r = np.arange(tm)
    tri = jnp.asarray((r[:, None] // GDN_CHUNK == r[None, :] // GDN_CHUNK) & (r[None, :] <= r[:, None]), BF16)
    hm = jax.ShapeDtypeStruct((b, nh, l, dk), F32)
    hm_spec = pl.BlockSpec((1, nh, tm, dk), lambda bi, i: (bi, 0, i, 0))
    halo_blocks = tm // SUBLANES
    return pl.pallas_call(
        functools.partial(_gdn_prep_body, tm=tm),
        grid=(b, l // tm),
        in_specs=[pl.BlockSpec((1, tm, wq), lambda bi, i: (bi, i, COL_GQKV // wq)),
                  pl.BlockSpec((1, SUBLANES, wq),
                               lambda bi, i: (bi, jnp.maximum(i * halo_blocks - 1, 0), COL_GQKV // wq)),
                  pl.BlockSpec((1, tm, LANES), lambda bi, i: (bi, i, COL_SMA // LANES)),
                  pl.BlockSpec((GDN_CONV, wq), lambda bi, i: (0, 0)),
                  pl.BlockSpec((1, LANES), lambda bi, i: (0, 0)),
                  pl.BlockSpec((1, LANES), lambda bi, i: (0, 0)),
                  pl.BlockSpec((tm, tm), lambda bi, i: (0, 0))],
        out_specs=[hm_spec] * 5 + [pl.BlockSpec((1, SUBLANES, tm), lambda bi, i: (bi, 0, i))],
        out_shape=[hm] * 5 + [jax.ShapeDtypeStruct((b, SUBLANES, l), F32)],
        scratch_shapes=[pltpu.VMEM((tm + SUBLANES, wq), F32)],
        compiler_params=_cparams("parallel", "arbitrary"),
        name="gdn_prep",
    )(proj3, proj3, proj3, conv_w, alog, dtb, tri)


def _split_bf16(a):
    hi = a.astype(BF16)
    lo = (a - hi.astype(F32)).astype(BF16)
    return hi, lo


def _bmm_split_rhs(a, b):
    ah = a.astype(BF16)
    bh, bl = _split_bf16(b)
    f = lambda u, v: jnp.einsum("nij,njk->nik", u, v, preferred_element_type=F32)
    return f(ah, bh) + f(ah, bl)


def _gdn_chunk_body(q_ref, k_ref, v_ref, beta_ref, gam_ref, gr_ref, z_ref, ng_ref, o_ref, s_ref,
                    *, nc):
    c = GDN_CHUNK
    nh, dk, dv = GDN_HEADS, GDN_DK, GDN_DV
    assert c == dk == dv

    @pl.when(pl.program_id(1) == 0)
    def _():
        s_ref[...] = jnp.zeros_like(s_ref)

    nb = nh * nc
    rows = lax.broadcasted_iota(jnp.int32, (nb, c, c), 1)
    cols = lax.broadcasted_iota(jnp.int32, (nb, c, c), 2)
    causal = rows >= cols
    strict = rows > cols
    q = q_ref[0].reshape(nb, c, dk)
    k = k_ref[0].reshape(nb, c, dk)
    v = v_ref[0].reshape(nb, c, dv)
    beta = beta_ref[0].reshape(nb, c, c)
    gc = gam_ref[0].reshape(nb, c, c)
    gr = jnp.concatenate([gr_ref[0, h:h + 1, n * c:(n + 1) * c].reshape(1, 1, c)
                          for h in range(nh) for n in range(nc)], axis=0)
    gr = jnp.broadcast_to(gr, (nb, c, c))
    decay = jnp.exp(jnp.where(causal, gc - gr, NEG_INF))
    kb = k.astype(BF16)
    qk = jnp.einsum("nid,njd->nij", jnp.concatenate([q.astype(BF16), kb], axis=1), kb,
                    preferred_element_type=F32)
    attn = qk[:, :c] * decay
    x = -jnp.where(strict, beta * qk[:, c:] * decay, 0.0)
    eg = jnp.exp(gc)
    uw = jnp.concatenate([beta * v, beta * eg * k], axis=2)
    levels = 6
    assert 2 ** levels == c
    for level in range(levels):
        if level < levels - 1:
            y = _bmm_split_rhs(x, jnp.concatenate([uw, x], axis=2))
            uw = uw + y[:, :, :2 * dv]
            x = y[:, :, 2 * dv:]
        else:
            uw = uw + _bmm_split_rhs(x, uw)
    g_last = gc[:, c - 1:c, :]
    split = lambda a: a.reshape((nh, nc) + a.shape[1:])
    u, w = split(uw[:, :, :dv]), split(uw[:, :, dv:])
    attn, q_dec = split(attn), split(q * eg)
    k_dec = split(k * jnp.exp(g_last - gc))
    gl = split(jnp.exp(g_last))
    ng = ng_ref[...]
    s = s_ref[...]
    for n in range(nc):
        sb = s.astype(BF16)
        r = jnp.einsum("hik,hkv->hiv", jnp.concatenate([w[:, n], q_dec[:, n]], axis=1).astype(BF16), sb,
                       preferred_element_type=F32)
        v_new = u[:, n] - r[:, :c]
        vb = v_new.astype(BF16)
        o = r[:, c:] + jnp.einsum("hij,hjv->hiv", attn[:, n].astype(BF16), vb, preferred_element_type=F32)
        upd = [lax.dot_general(k_dec[h, n].astype(BF16), vb[h], (((0,), (0,)), ((), ())),
                               preferred_element_type=F32) for h in range(nh)]
        s = gl[:, n] * s + jnp.stack(upd, axis=0)
        o = o * lax.rsqrt(jnp.mean(o * o, axis=-1, keepdims=True) + EPS) * ng
        o_all = jnp.concatenate([o[h] for h in range(nh)], axis=1)
        o_ref[0, n * c:(n + 1) * c, :] = (o_all * _silu(z_ref[0, n * c:(n + 1) * c, :])).astype(o_ref.dtype)
    s_ref[...] = s


def gdn_chunk(q, k, v, beta, gam, gr, proj3, norm_g, lt):
    b, nh, l, dk = q.shape
    nc = lt // GDN_CHUNK
    wz = nh * GDN_DV
    hm_spec = pl.BlockSpec((1, nh, lt, dk), lambda bi, i: (bi, 0, i, 0))
    return pl.pallas_call(
        functools.partial(_gdn_chunk_body, nc=nc),
        grid=(b, l // lt),
        in_specs=[hm_spec] * 5 + [
            pl.BlockSpec((1, SUBLANES, lt), lambda bi, i: (bi, 0, i)),
            pl.BlockSpec((1, lt, wz), lambda bi, i: (bi, i, COL_AZ // wz)),
            pl.BlockSpec((1, GDN_DV), lambda bi, i: (0, 0))],
        out_specs=pl.BlockSpec((1, lt, wz), lambda bi, i: (bi, i, 0)),
        out_shape=jax.ShapeDtypeStruct((b, l, wz), BF16),
        scratch_shapes=[pltpu.VMEM((nh, GDN_DK, GDN_DV), F32)],
        compiler_params=_cparams("parallel", "arbitrary"),
        name="gdn_chunk",
    )(q, k, v, beta, gam, gr, proj3, norm_g.reshape(1, GDN_DV))


MASK_BIAS = -1e30
LOG2E = 1.4426950408889634


def _moba_gate_body(q_ref, kall_ref, k_ref, v_ref, pq_ref, pb_ref, qa_ref, ka_ref, vb_ref, km_ref, *, nb):
    i = pl.program_id(1)
    nh, hd = MOBA_HEADS, MOBA_HD
    w = nh * hd

    assert nb & (nb - 1) == 0 and hd & (hd - 1) == 0
    nb_bits, hd_bits = nb.bit_length() - 1, hd.bit_length() - 1

    @pl.when(i == 0)
    def _():
        kmean = jnp.mean(kall_ref[0].reshape(nb, MOBA_BLOCK, w), axis=1)
        lane = lax.broadcasted_iota(jnp.int32, (nb, w), 1)
        for h in range(nh):
            km_ref[h * nb:(h + 1) * nb, :] = jnp.where(jnp.right_shift(lane, hd_bits) == h, kmean, 0.0)

    q = q_ref[0]
    gate = lax.dot_general(km_ref[...], q, (((1,), (1,)), ((), ())),
                           precision=HIGHEST, preferred_element_type=F32)
    blk = lax.broadcasted_iota(jnp.int32, (nb, gate.shape[1]), 0)
    valid = blk < i
    big = jnp.int32(nb)
    masked = []
    for h in range(nh):
        g_h = gate[h * nb:(h + 1) * nb, :]
        avail = valid
        for _ in range(MOBA_TOPK):
            m = jnp.max(jnp.where(avail, g_h, NEG_INF), axis=0, keepdims=True)
            cand = avail & (g_h == m)
            first = jnp.min(jnp.where(cand, blk, big), axis=0, keepdims=True)
            avail = avail & (blk != first)
        masked.append(avail.astype(F32))
    masked = jnp.concatenate(masked, axis=0).astype(BF16)
    bias = lax.dot_general(masked, pb_ref[...], (((0,), (0,)), ((), ())), preferred_element_type=F32)
    q_tiles = _dot(q * (hd ** -0.5 * LOG2E), pq_ref[...]) + bias * MASK_BIAS
    qa_ref[0] = q_tiles.astype(BF16)
    tile_lane = jnp.bitwise_and(lax.broadcasted_iota(jnp.int32, q_tiles.shape, 1), LANES - 1)
    ka_ref[0] = (_dot(k_ref[0], pq_ref[...]) + (tile_lane == hd + i).astype(F32)).astype(BF16)
    vb_ref[0] = v_ref[0].astype(BF16)


def moba_gate(proj3):
    b, l, _ = proj3.shape
    nh, hd = MOBA_HEADS, MOBA_HD
    w = nh * hd
    nb = l // MOBA_BLOCK
    tq = MOBA_BLOCK
    assert hd + nb <= LANES
    src = np.arange(w)
    place_q = np.zeros((w, nh * LANES), np.float32)
    place_q[src, (src // hd) * LANES + src % hd] = 1.0
    srcb = np.arange(nh * nb)
    place_b = np.zeros((nh * nb, nh * LANES), np.float32)
    place_b[srcb, (srcb // nb) * LANES + hd + srcb % nb] = 1.0
    place_q, place_b = jnp.asarray(place_q, BF16), jnp.asarray(place_b, BF16)
    cq = COL_MQKV // w
    row = lambda c: pl.BlockSpec((1, tq, w), lambda bi, i: (bi, i, c))
    full = lambda a: pl.BlockSpec(a.shape, lambda bi, i: (0, 0))
    tiles = jax.ShapeDtypeStruct((b, l, nh * LANES), BF16)
    tspec = pl.BlockSpec((1, tq, nh * LANES), lambda bi, i: (bi, i, 0))
    return pl.pallas_call(
        functools.partial(_moba_gate_body, nb=nb),
        grid=(b, l // tq),
        in_specs=[row(cq), pl.BlockSpec((1, l, w), lambda bi, i: (bi, 0, cq + 1)), row(cq + 1), row(cq + 2),
                  full(place_q), full(place_b)],
        out_specs=[tspec, tspec, pl.BlockSpec((1, tq, w), lambda bi, i: (bi, i, 0))],
        out_shape=[tiles, tiles, jax.ShapeDtypeStruct((b, l, w), BF16)],
        scratch_shapes=[pltpu.VMEM((nh * nb, w), F32)],
        compiler_params=_cparams("parallel", "arbitrary"),
        name="moba_gate",
    )(proj3, proj3, proj3, proj3, place_q, place_b)


M_INIT = -2e38


def _flash_body(q_ref, k_ref, v_ref, o_ref, m_scr, acc_scr, *, nh, tq, tk):
    i = pl.program_id(1)
    row = lax.broadcasted_iota(jnp.int32, (tq, tk), 0)
    col = lax.broadcasted_iota(jnp.int32, (tq, tk), 1)
    ones = jnp.ones((tk, LANES), BF16)
    m_scr[...] = jnp.full(m_scr.shape, M_INIT, F32)
    acc_scr[...] = jnp.zeros(acc_scr.shape, F32)

    def tiles(kv_r0, diag):
        for h in range(nh):
            q = q_ref[0, :, h * LANES:(h + 1) * LANES]
            kt = k_ref[0, pl.ds(kv_r0, tk), h * LANES:(h + 1) * LANES]
            v_lo = (h // 2) * LANES
            vt = jnp.concatenate([v_ref[0, pl.ds(kv_r0, tk), v_lo:v_lo + LANES], ones], axis=1)
            s = _dot_nt(q, kt)
            if diag is not None:
                s = jnp.where(col + diag * tk <= row, s, MASK_BIAS)
            m_prev = m_scr[h]
            m_next = jnp.maximum(m_prev, jnp.max(s, axis=1, keepdims=True))
            p = jnp.exp2(s - jnp.concatenate([m_next] * (tk // LANES), axis=1))
            alpha = jnp.exp2(m_prev - m_next)
            acc_scr[h] = jnp.concatenate([alpha, alpha], axis=1) * acc_scr[h] + _dot(p, vt)
            m_scr[h] = m_next

    def past(j, carry):
        tiles(pl.multiple_of(j * tk, tk), None)
        return carry

    lax.fori_loop(0, i * (tq // tk), past, 0)
    for d in range(tq // tk):
        tiles(pl.multiple_of(i * tq + d * tk, tk), d)
    lane = lax.broadcasted_iota(jnp.int32, (tq, LANES), 1)
    low = lane < LANES // 2
    outs = [acc_scr[h][:, :LANES] / acc_scr[h][:, LANES:] for h in range(nh)]
    pairs = [jnp.where(low, outs[2 * p], outs[2 * p + 1]) for p in range(nh // 2)]
    o_ref[0] = jnp.concatenate(pairs, axis=1).astype(o_ref.dtype)


def flash_attention(q, k, v, *, nh, name):
    b, l, wq = q.shape
    wv = v.shape[2]
    tq = tk = 2 * MOBA_BLOCK
    return pl.pallas_call(
        functools.partial(_flash_body, nh=nh, tq=tq, tk=tk),
        grid=(b, l // tq),
        in_specs=[pl.BlockSpec((1, tq, wq), lambda bi, i: (bi, i, 0)),
                  pl.BlockSpec((1, l, wq), lambda bi, i: (bi, 0, 0)),
                  pl.BlockSpec((1, l, wv), lambda bi, i: (bi, 0, 0))],
        out_specs=pl.BlockSpec((1, tq, wv), lambda bi, i: (bi, i, 0)),
        out_shape=jax.ShapeDtypeStruct((b, l, wv), BF16),
        scratch_shapes=[pltpu.VMEM((nh, tq, LANES), F32), pltpu.VMEM((nh, tq, 2 * LANES), F32)],
        compiler_params=_cparams("parallel", "arbitrary"),
        name=name,
    )(q, k, v)


def _mla_proj_body(cq_ref, ckv_ref, sma_ref, smb_ref, qg_ref, kvg_ref, wq_ref, wkv_ref,
                   cq_t, sq_t, ck_t, sk_t, q_ref, k_ref, v_ref):
    nh = MLA_HEADS
    cq = cq_ref[0]
    hq = cq * lax.rsqrt(jnp.sum(cq * cq, axis=-1, keepdims=True) * (1.0 / MLA_Q_LORA) + EPS) * qg_ref[...]
    qq = _dot(hq, wq_ref[...])
    half = nh * LANES
    q_ref[0] = (qq[:, :half] * cq_t[...] + qq[:, half:] * sq_t[...]).astype(BF16)
    ckv = ckv_ref[0]
    hkv = ckv * lax.rsqrt(jnp.mean(ckv * ckv, axis=-1, keepdims=True) + EPS) * kvg_ref[...]
    kv = _dot(hkv, wkv_ref[...])
    krot = sma_ref[0] * ck_t[...] + smb_ref[0] * sk_t[...]
    k_ref[0] = (kv[:, :half] + jnp.concatenate([krot] * nh, axis=1)).astype(BF16)
    v_ref[0] = kv[:, half:].astype(BF16)


def mla_weights(q_norm_g, w_uq, kv_norm_g, w_ukv):
    nh, dn, dr, dvh = MLA_HEADS, MLA_NOPE, MLA_ROPE, MLA_V
    hr = dr // 2
    cqw = COL_CKV - COL_CQ
    w1 = jnp.zeros((cqw, nh, LANES), F32)
    w2 = jnp.zeros((cqw, nh, LANES), F32)
    wq3 = w_uq.reshape(MLA_Q_LORA, nh, dn + dr)
    w1 = w1.at[:MLA_Q_LORA, :, :dn + dr].set(wq3)
    w2 = w2.at[:MLA_Q_LORA, :, dn:dn + hr].set(-wq3[:, :, dn + hr:])
    w2 = w2.at[:MLA_Q_LORA, :, dn + hr:dn + dr].set(wq3[:, :, dn:dn + hr])
    wq = jnp.concatenate([w1.reshape(cqw, nh * LANES), w2.reshape(cqw, nh * LANES)], axis=1).astype(BF16)
    qg = jnp.zeros((1, cqw), F32).at[0, :MLA_Q_LORA].set(q_norm_g)
    wkv3 = w_ukv.reshape(MLA_KV_LORA, nh, dn + dvh)
    wk = jnp.zeros((MLA_KV_LORA, nh, LANES), F32).at[:, :, :dn].set(wkv3[:, :, :dn])
    wkv = jnp.concatenate([wk.reshape(MLA_KV_LORA, nh * LANES),
                           wkv3[:, :, dn:].reshape(MLA_KV_LORA, nh * dvh)], axis=1).astype(BF16)
    return qg, kv_norm_g.reshape(1, MLA_KV_LORA), wq, wkv


def mla_tables(cos, sin):
    l = cos.shape[0]
    nh, dn, dr = MLA_HEADS, MLA_NOPE, MLA_ROPE
    scale = (dn + dr) ** -0.5 * LOG2E
    cc = jnp.concatenate([cos, cos], axis=1)
    ss = jnp.concatenate([sin, sin], axis=1)
    pad = jnp.zeros((l, LANES - dn - dr), F32)
    cq_tile = jnp.concatenate([jnp.ones((l, dn), F32), cc, pad], axis=1) * scale
    sq_tile = jnp.concatenate([jnp.zeros((l, dn), F32), ss, pad], axis=1) * scale
    cq_t = jnp.tile(cq_tile, (1, nh))
    sq_t = jnp.tile(sq_tile, (1, nh))
    ck_t = jnp.concatenate([jnp.zeros((l, dn), F32), cc, pad], axis=1)
    sk_t = jnp.concatenate([jnp.zeros((l, dn), F32), ss, pad], axis=1)
    return cq_t, sq_t, ck_t, sk_t


def mla_proj(proj3, weights, tables, tm):
    b, l, _ = proj3.shape
    nh, dvh = MLA_HEADS, MLA_V
    cqw = COL_CKV - COL_CQ
    qg, kvg, wq, wkv = weights
    cq_t, sq_t, ck_t, sk_t = tables
    row = lambda w_, c_: pl.BlockSpec((1, tm, w_), lambda bi, i: (bi, i, c_ // w_))
    full = lambda a: pl.BlockSpec(a.shape, lambda bi, i: (0, 0))
    tab = lambda w_: pl.BlockSpec((tm, w_), lambda bi, i: (i, 0))
    out_q = jax.ShapeDtypeStruct((b, l, nh * LANES), BF16)
    out_v = jax.ShapeDtypeStruct((b, l, nh * dvh), BF16)
    return pl.pallas_call(
        _mla_proj_body,
        grid=(b, l // tm),
        in_specs=[row(cqw, COL_CQ), row(MLA_KV_LORA, COL_CKV), row(LANES, COL_SMA), row(LANES, COL_SMB),
                  full(qg), full(kvg), full(wq), full(wkv),
                  tab(nh * LANES), tab(nh * LANES), tab(LANES), tab(LANES)],
        out_specs=[pl.BlockSpec((1, tm, nh * LANES), lambda bi, i: (bi, i, 0)),
                   pl.BlockSpec((1, tm, nh * LANES), lambda bi, i: (bi, i, 0)),
                   pl.BlockSpec((1, tm, nh * dvh), lambda bi, i: (bi, i, 0))],
        out_shape=[out_q, out_q, out_v],
        compiler_params=_cparams("parallel", "parallel"),
        name="mla_proj",
    )(proj3, proj3, proj3, proj3, qg, kvg, wq, wkv, cq_t, sq_t, ck_t, sk_t)


def _s5_state_body(u_ref, pst_ref, lam_ref, xp_ref, pu_scr, xp_scr, *, nch, nseq):
    half = S5_GROUPS * S5_STATE
    pu_scr[...] = _dot(u_ref[...], pst_ref[...])
    ar = lam_ref[0:1, :]
    ai = lam_ref[1:2, :]

    def step(kc, carry):
        nxt = []
        for sq in range(nseq):
            xr, xi = carry[sq]
            r = sq * nch + kc
            xp_scr[pl.ds(r, 1), :] = jnp.concatenate([xr, xi], axis=1)
            e = pu_scr[pl.ds(r, 1), :]
            nxt.append((ar * xr - ai * xi + e[:, :half], ar * xi + ai * xr + e[:, half:]))
        return tuple(nxt)

    zero = jnp.zeros((1, half), F32)
    lax.fori_loop(0, nch, step, tuple((zero, zero) for _ in range(nseq)))
    xp_ref[...] = xp_scr[...].astype(BF16)


def _s5_out_body(u_ref, xp_ref, kr_ref, qst_ref, y_ref):
    cn, w = S5_CHUNK, S5_WIDTH
    xp = xp_ref[...]
    for t in range(cn):
        y_ref[:, t * w:(t + 1) * w] = (_dot(u_ref[:, :(t + 1) * w], kr_ref[(cn - 1 - t) * w:, :])
                                       + _dot(xp, qst_ref[:, t * w:(t + 1) * w]))


def s5_weights(lam_re, lam_im, b_re, b_im, c_re, c_im, d, log_dt):
    cn = S5_CHUNK
    g, p, cg = S5_GROUPS, S5_STATE, S5_GROUP
    ein = functools.partial(jnp.einsum, precision=HIGHEST)
    eye = jnp.eye(g, dtype=F32)
    lr = jnp.minimum(lam_re, -1e-4)
    li = lam_im
    dt = jnp.exp(log_dt)[:, None]
    mag = jnp.exp(lr * dt)
    ar, ai = mag * jnp.cos(li * dt), mag * jnp.sin(li * dt)
    den = lr * lr + li * li
    cr = ((ar - 1.0) * lr + ai * li) / den
    ci = (ai * lr - (ar - 1.0) * li) / den
    bbr = cr[..., None] * b_re - ci[..., None] * b_im
    bbi = cr[..., None] * b_im + ci[..., None] * b_re
    def powers(taus):
        tau = jnp.asarray(taus, F32)[None, :, None]
        pmag = jnp.exp((lr * dt)[:, None, :] * tau)
        pang = (li * dt)[:, None, :] * tau
        return pmag * jnp.cos(pang), pmag * jnp.sin(pang)

    def c_times(pwr, pwi):
        return (c_re[:, None] * pwr[:, :, None, :] - c_im[:, None] * pwi[:, :, None, :],
                c_re[:, None] * pwi[:, :, None, :] + c_im[:, None] * pwr[:, :, None, :])

    dsc_r, dsc_i = powers(np.arange(cn - 1, -1, -1))
    asc_r, asc_i = powers(np.arange(1, cn + 1))
    clr, cli = c_times(dsc_r, dsc_i)
    kt = ein("gtap,gpc->gtac", clr, bbr) - ein("gtap,gpc->gtac", cli, bbi)
    kt = kt.at[:, cn - 1].add(d[:, :, None] * jnp.eye(cg, dtype=F32)[None])
    kr = jnp.einsum("gtac,gh->tgcha", kt, eye).reshape(cn * g * cg, g * cg)
    bbr_t, bbi_t = bbr.transpose(0, 2, 1)[:, None], bbi.transpose(0, 2, 1)[:, None]
    pr = dsc_r[:, :, None, :] * bbr_t - dsc_i[:, :, None, :] * bbi_t
    pi = dsc_r[:, :, None, :] * bbi_t + dsc_i[:, :, None, :] * bbr_t
    pst = jnp.stack([jnp.einsum("gscp,gh->sgchp", m, eye) for m in (pr, pi)], axis=3)
    pst = pst.reshape(cn * g * cg, 2 * g * p)
    clr, cli = c_times(asc_r, asc_i)
    qst = jnp.stack([jnp.einsum("gtap,gh->gptha", m, eye) for m in (clr, -cli)], axis=0)
    qst = qst.reshape(2 * g * p, cn * g * cg)
    lam = jnp.stack([asc_r[:, cn - 1].reshape(g * p), asc_i[:, cn - 1].reshape(g * p)], axis=0)
    return kr.astype(BF16), pst.astype(BF16), qst.astype(BF16), lam


def s5_scan(su, weights, batch):
    t, w = su.shape
    cn = S5_CHUNK
    kr, pst, qst, lam = weights
    nr = t // cn
    nch = nr // batch
    nseq = 2
    rows = nseq * nch
    u = su.reshape(nr, cn * w)
    ns = pst.shape[1]
    once = lambda a: pl.BlockSpec(a.shape, lambda i: (0, 0), pipeline_mode=pl.Buffered(1))
    rowb = lambda width: pl.BlockSpec((rows, width), lambda i: (i, 0))
    xp = pl.pallas_call(
        functools.partial(_s5_state_body, nch=nch, nseq=nseq),
        grid=(nr // rows,),
        in_specs=[rowb(cn * w), once(pst), once(lam)],
        out_specs=rowb(ns),
        out_shape=jax.ShapeDtypeStruct((nr, ns), BF16),
        scratch_shapes=[pltpu.VMEM((rows, ns), F32)] * 2,
        compiler_params=_cparams("parallel"),
        name="s5_state",
    )(u, pst, lam)
    y = pl.pallas_call(
        _s5_out_body,
        grid=(nr // rows,),
        in_specs=[rowb(cn * w), rowb(ns), once(kr), once(qst)],
        out_specs=rowb(cn * w),
        out_shape=jax.ShapeDtypeStruct((nr, cn * w), F32),
        compiler_params=_cparams("parallel"),
        name="s5_out",
    )(u, xp, kr, qst)
    return y.reshape(t, w)


def _out_proj_body(x_ref, oa_ref, ob_ref, oc_ref, ys_ref, gw_ref, gb_ref, wo_ref, o_ref):
    y = jax.nn.gelu(ys_ref[...])
    od = y * _sigmoid(_dot(y, gw_ref[...]) + gb_ref[...])
    mix = jnp.concatenate([oa_ref[...].astype(BF16), ob_ref[...], oc_ref[...], od.astype(BF16)], axis=1)
    o_ref[...] = x_ref[...] + _dot(mix, wo_ref[...])


def out_proj(x, oa, ob, oc, ys, glu_w, glu_b, w_out, tm):
    t, dm = x.shape
    wm = oa.shape[1]
    row = lambda w_: pl.BlockSpec((tm, w_), lambda i: (i, 0))
    full = lambda a: pl.BlockSpec(a.shape, lambda i: (0, 0))
    gb = glu_b.reshape(1, wm)
    return pl.pallas_call(
        _out_proj_body,
        grid=(t // tm,),
        in_specs=[row(dm), row(wm), row(wm), row(wm), row(wm), full(glu_w), full(gb), full(w_out)],
        out_specs=row(dm),
        out_shape=jax.ShapeDtypeStruct((t, dm), F32),
        compiler_params=_cparams("parallel"),
        name="out_proj",
    )(x, oa, ob, oc, ys, glu_w, gb, w_out)


def _ffn_body(x_ref, halo_ref, g_ref, wup_ref, cw_ref, wdn_ref, fg_ref, o_ref, up_scr, *, tm, fc, final_norm):
    i = pl.program_id(1)
    keep = (i > 0).astype(F32)

    def norm(v, g):
        return v * lax.rsqrt(jnp.mean(v * v, axis=-1, keepdims=True) + EPS) * g

    x = x_ref[0]
    h_main = norm(x, g_ref[...]).astype(BF16)
    h_halo = norm(halo_ref[0], g_ref[...]).astype(BF16)
    acc = x
    for c in range(D_FF // fc):
        acts = []
        for part in range(2):
            lo = part * D_FF + c * fc
            up_scr[0:SUBLANES, :] = _dot(h_halo, wup_ref[:, lo:lo + fc]) * keep
            up_scr[SUBLANES:, :] = _dot(h_main, wup_ref[:, lo:lo + fc])
            w = cw_ref[:, lo:lo + fc]
            y = w[FFN_CONV - 1:FFN_CONV, :] * up_scr[pl.ds(SUBLANES, tm), :]
            for tap in range(FFN_CONV - 1):
                y = y + w[tap:tap + 1, :] * up_scr[pl.ds(SUBLANES - (FFN_CONV - 1) + tap, tm), :]
            acts.append(y)
        act = _silu(acts[0]) * acts[1]
        acc = acc + _dot(act, wdn_ref[c * fc:(c + 1) * fc, :])
    o_ref[0] = norm(acc, fg_ref[...]) if final_norm else acc


def ffn(x3, norm_g, w_up, conv_w, w_down, final_g, tm, fc):
    b, l, dm = x3.shape
    halo_blocks = tm // SUBLANES
    once = lambda a: pl.BlockSpec(a.shape, lambda bi, i: (0, 0), pipeline_mode=pl.Buffered(1))
    g = norm_g.reshape(1, dm)
    final_norm = final_g is not None
    fg = final_g.reshape(1, dm) if final_norm else g
    return pl.pallas_call(
        functools.partial(_ffn_body, tm=tm, fc=fc, final_norm=final_norm),
        grid=(b, l // tm),
        in_specs=[pl.BlockSpec((1, tm, dm), lambda bi, i: (bi, i, 0)),
                  pl.BlockSpec((1, SUBLANES, dm), lambda bi, i: (bi, jnp.maximum(i * halo_blocks - 1, 0), 0)),
                  once(g), once(w_up), once(conv_w), once(w_down), once(fg)],
        out_specs=pl.BlockSpec((1, tm, dm), lambda bi, i: (bi, i, 0)),
        out_shape=jax.ShapeDtypeStruct((b, l, dm), F32),
        scratch_shapes=[pltpu.VMEM((tm + SUBLANES, fc), F32)],
        compiler_params=_cparams("parallel", "arbitrary"),
        name="ffn",
    )(x3, x3, g, w_up, conv_w, w_down, fg)


def _permute_w_in(w_in):
    dm = w_in.shape[0]
    gw, mw = GDN_HEADS * GDN_DK, MOBA_HEADS * MOBA_HD
    o_aq, o_az, o_aa, o_ab = 0, 3 * gw, 4 * gw, 4 * gw + GDN_HEADS
    o_mq = o_ab + GDN_HEADS
    o_cq = o_mq + 3 * mw
    o_ckv = o_cq + MLA_Q_LORA
    o_ckr = o_ckv + MLA_KV_LORA
    o_su = o_ckr + MLA_ROPE
    hr = MLA_ROPE // 2
    z = lambda n: jnp.zeros((dm, n), w_in.dtype)
    ckr = w_in[:, o_ckr:o_ckr + MLA_ROPE]
    ckr_swap = jnp.concatenate([-ckr[:, hr:], ckr[:, :hr]], axis=1)
    cols = [w_in[:, o_aq:o_aq + 3 * gw],
            w_in[:, o_mq:o_mq + 3 * mw],
            w_in[:, o_az:o_az + gw],
            w_in[:, o_su:o_su + S5_WIDTH],
            w_in[:, o_cq:o_cq + MLA_Q_LORA], z(COL_CKV - COL_CQ - MLA_Q_LORA),
            w_in[:, o_ckv:o_ckv + MLA_KV_LORA],
            w_in[:, o_aa:o_aa + 2 * GDN_HEADS], z(MLA_NOPE - 2 * GDN_HEADS), ckr, z(LANES - MLA_NOPE - MLA_ROPE),
            z(MLA_NOPE), ckr_swap, z(LANES - MLA_NOPE - MLA_ROPE)]
    return jnp.concatenate(cols, axis=1).astype(BF16)


def _layer(x3, tables, w, final_g):
    b, l, dm = x3.shape
    t = b * l
    x2 = x3.reshape(t, dm)
    proj, su = in_proj(x2, w["norm1_g"], w["w_in"], ROW_TILE)
    proj3 = proj.reshape(b, l, N_PROJ)

    q, k, v, beta, gam, gr = gdn_prep(proj3, w["gdn_conv_w"], w["gdn_alog"], w["gdn_dtb"], ROW_TILE)
    o_a = gdn_chunk(q, k, v, beta, gam, gr, proj3, w["gdn_norm_g"], ROW_TILE).reshape(t, GDN_HEADS * GDN_DV)

    mq, mk, mv = moba_gate(proj3)
    o_b = flash_attention(mq, mk, mv, nh=MOBA_HEADS, name="flash_moba").reshape(t, MOBA_HEADS * MOBA_HD)

    cq, ck, cv = mla_proj(proj3, w["mla"], tables, ROW_TILE)
    o_c = flash_attention(cq, ck, cv, nh=MLA_HEADS, name="flash_mla").reshape(t, MLA_HEADS * MLA_V)

    ys = s5_scan(su, w["s5"], b)

    x2 = out_proj(x2, o_a, o_b, o_c, ys, w["s5_glu_w"], w["s5_glu_b"], w["w_out"], ROW_TILE)
    return ffn(x2.reshape(b, l, dm), w["norm2_g"], w["ffn_w_up"], w["ffn_conv_w"], w["ffn_w_down"], final_g,
               FFN_ROW_TILE, D_FF // 2)


def _prepare_weights(p):
    nh = GDN_HEADS
    return {
        "norm1_g": p["norm1_g"], "w_in": _permute_w_in(p["w_in"]),
        "gdn_conv_w": p["gdn_conv_w"], "gdn_norm_g": p["gdn_norm_g"],
        "gdn_alog": jnp.zeros((1, LANES), F32).at[0, :nh].set(p["gdn_a_log"]),
        "gdn_dtb": jnp.zeros((1, LANES), F32).at[0, :nh].set(p["gdn_dt_bias"]),
        "mla": mla_weights(p["mla_q_norm_g"], p["mla_w_uq"], p["mla_kv_norm_g"], p["mla_w_ukv"]),
        "s5": s5_weights(p["s5_lam_re"], p["s5_lam_im"], p["s5_b_re"], p["s5_b_im"], p["s5_c_re"], p["s5_c_im"],
                         p["s5_d"], p["s5_log_dt"]),
        "s5_glu_w": p["s5_glu_w"].astype(BF16), "s5_glu_b": p["s5_glu_b"], "w_out": p["w_out"].astype(BF16),
        "norm2_g": p["norm2_g"], "ffn_w_up": p["ffn_w_up"].astype(BF16), "ffn_conv_w": p["ffn_conv_w"],
        "ffn_w_down": p["ffn_w_down"].astype(BF16),
    }


_LAYER_KEYS = ("norm1_g", "w_in", "gdn_conv_w", "gdn_a_log", "gdn_dt_bias", "gdn_norm_g",
               "mla_q_norm_g", "mla_w_uq", "mla_kv_norm_g", "mla_w_ukv",
               "s5_lam_re", "s5_lam_im", "s5_b_re", "s5_b_im", "s5_c_re", "s5_c_im", "s5_d", "s5_log_dt",
               "s5_glu_w", "s5_glu_b", "w_out", "norm2_g", "ffn_w_up", "ffn_conv_w", "ffn_w_down")


def kernel(x, norm1_g, w_in, gdn_conv_w, gdn_a_log, gdn_dt_bias, gdn_norm_g, mla_q_norm_g, mla_w_uq,
           mla_kv_norm_g, mla_w_ukv, s5_lam_re, s5_lam_im, s5_b_re, s5_b_im, s5_c_re, s5_c_im, s5_d,
           s5_log_dt, s5_glu_w, s5_glu_b, w_out, norm2_g, ffn_w_up, ffn_conv_w, ffn_w_down, final_norm_g):
    stacked = (norm1_g, w_in, gdn_conv_w, gdn_a_log, gdn_dt_bias, gdn_norm_g, mla_q_norm_g, mla_w_uq,
               mla_kv_norm_g, mla_w_ukv, s5_lam_re, s5_lam_im, s5_b_re, s5_b_im, s5_c_re, s5_c_im, s5_d,
               s5_log_dt, s5_glu_w, s5_glu_b, w_out, norm2_g, ffn_w_up, ffn_conv_w, ffn_w_down)
    l = x.shape[1]
    inv_freq = ROPE_THETA ** (-jnp.arange(0, MLA_ROPE, 2, dtype=F32) / MLA_ROPE)
    ang = jnp.arange(l, dtype=F32)[:, None] * inv_freq[None, :]
    tables = mla_tables(jnp.cos(ang), jnp.sin(ang))
    weights = jax.vmap(_prepare_weights)(dict(zip(_LAYER_KEYS, stacked)))
    depth = norm1_g.shape[0]
    for i in range(depth):
        w = jax.tree_util.tree_map(lambda a: a[i], weights)
        x = _layer(x, tables, w, final_norm_g if i == depth - 1 else None)
    return x
```

```python
import functools
import math

import jax
import jax.numpy as jnp
import numpy as np
from jax import lax
from jax.experimental import pallas as pl
from jax.experimental.pallas import tpu as pltpu

F32 = jnp.float32
BF16 = jnp.bfloat16
NEG_INF = float("-inf")
EPS = 1e-6
HIGHEST = lax.Precision.HIGHEST

D_MODEL = 1024
GDN_HEADS, GDN_DK, GDN_DV, GDN_CONV, GDN_CHUNK = 4, 64, 64, 4, 64
MOBA_HEADS, MOBA_HD, MOBA_BLOCK, MOBA_TOPK = 4, 64, 256, 3
MLA_HEADS, MLA_Q_LORA, MLA_KV_LORA, MLA_NOPE, MLA_ROPE, MLA_V = 4, 192, 128, 64, 32, 64
ROPE_THETA = 10000.0
S5_GROUP, S5_GROUPS, S5_STATE = 16, 16, 64
S5_WIDTH = S5_GROUP * S5_GROUPS
D_FF, FFN_CONV = 2816, 3

LANES = 128
SUBLANES = 8
VMEM_LIMIT = 56 * 1024 * 1024

COL_GQKV, COL_MQKV, COL_AZ, COL_SU, COL_CQ, COL_CKV, COL_SMA, COL_SMB = (
    0, 768, 1536, 1792, 2048, 2304, 2432, 2560)
N_PROJ = 2688
S5_CHUNK = 16
ROW_TILE = 512
FFN_ROW_TILE = 1024


def _cparams(*sems):
    return pltpu.CompilerParams(dimension_semantics=sems, vmem_limit_bytes=VMEM_LIMIT)


def _dot(a, b):
    return jnp.dot(a.astype(BF16), b.astype(BF16), preferred_element_type=F32)


def _dot_nt(a, b):
    return lax.dot_general(a.astype(BF16), b.astype(BF16), (((1,), (1,)), ((), ())),
                           preferred_element_type=F32)


def _split3_bf16(x):
    hi = x.astype(BF16)
    r1 = x - hi.astype(F32)
    mid = r1.astype(BF16)
    return hi, mid, (r1 - mid.astype(F32)).astype(BF16)


def _sigmoid(x):
    return jax.nn.sigmoid(x)


def _silu(x):
    return x * jax.nn.sigmoid(x)


def _in_proj_body(x_ref, g_ref, w_ref, o_ref, su_ref, *, inv_dim):
    x = x_ref[...]
    ms = jnp.sum(x * x, axis=-1, keepdims=True) * inv_dim
    h = x * lax.rsqrt(ms + EPS) * g_ref[...]
    proj = _dot(h, w_ref[...])
    o_ref[...] = proj
    su_ref[...] = proj[:, COL_SU:COL_SU + S5_WIDTH].astype(BF16)


def in_proj(x, g, w, tm):
    t, k = x.shape
    n = w.shape[1]
    return pl.pallas_call(
        functools.partial(_in_proj_body, inv_dim=1.0 / k),
        grid=(t // tm,),
        in_specs=[pl.BlockSpec((tm, k), lambda i: (i, 0)),
                  pl.BlockSpec((1, k), lambda i: (0, 0)),
                  pl.BlockSpec((k, n), lambda i: (0, 0))],
        out_specs=[pl.BlockSpec((tm, n), lambda i: (i, 0)), pl.BlockSpec((tm, S5_WIDTH), lambda i: (i, 0))],
        out_shape=[jax.ShapeDtypeStruct((t, n), F32), jax.ShapeDtypeStruct((t, S5_WIDTH), BF16)],
        compiler_params=_cparams("parallel"),
        name="in_proj",
    )(x, g.reshape(1, k), w)


def _gdn_prep_body(qkv_ref, halo_ref, sm_ref, cw_ref, alog_ref, dtb_ref, tri_ref, eb_ref, eg_ref,
                   q_ref, k_ref, v_ref, beta_ref, gam_ref, gr_ref, xs_ref, *, tm):
    i = pl.program_id(1)
    keep = (i > 0).astype(F32)
    xs_ref[0:SUBLANES, :] = halo_ref[0] * keep
    xs_ref[SUBLANES:, :] = qkv_ref[0]
    w = cw_ref[...]
    nq = GDN_HEADS * GDN_DK
    dk = GDN_DK
    y = w[3:4, :] * xs_ref[pl.ds(SUBLANES, tm), :]
    for tap in range(GDN_CONV - 1):
        y = y + w[tap:tap + 1, :] * xs_ref[pl.ds(SUBLANES - (GDN_CONV - 1) + tap, tm), :]
    y = _silu(y)
    for h in range(GDN_HEADS):
        lanes = slice(h * dk, (h + 1) * dk)
        qh = y[:, h * dk:(h + 1) * dk]
        kh = y[:, nq + h * dk:nq + (h + 1) * dk]
        q_ref[0, :, lanes] = qh * lax.rsqrt(jnp.sum(qh * qh, axis=-1, keepdims=True) + EPS) * (dk ** -0.5)
        k_ref[0, :, lanes] = kh * lax.rsqrt(jnp.sum(kh * kh, axis=-1, keepdims=True) + EPS)
    v_ref[0] = y[:, 2 * nq:]
    sm = sm_ref[0]
    g = -jnp.exp(alog_ref[...]) * jax.nn.softplus(sm + dtb_ref[...])
    beta = _sigmoid(sm)
    tri = tri_ref[...]
    gam = sum(jnp.dot(tri, part, preferred_element_type=F32) for part in _split3_bf16(g))
    pick = (lax.broadcasted_iota(jnp.int32, (SUBLANES, LANES), 0)
            == lax.broadcasted_iota(jnp.int32, (SUBLANES, LANES), 1)).astype(BF16)
    gr_ref[0] = sum(lax.dot_general(pick, part, (((1,), (1,)), ((), ())), preferred_element_type=F32)
                    for part in _split3_bf16(gam))
    beta_ref[0] = sum(jnp.dot(part, eb_ref[...], preferred_element_type=F32) for part in _split3_bf16(beta))
    gam_ref[0] = sum(jnp.dot(part, eg_ref[...], preferred_element_type=F32) for part in _split3_bf16(gam))


def gdn_prep(proj3, conv_w, alog, dtb, tm):
    b, l, _ = proj3.shape
    nh, dk = GDN_HEADS, GDN_DK
    nq = nh * dk
    wq = 3 * nq
    r = np.arange(tm)
    tri = jnp.asarray((r[:, None] // GDN_CHUNK == r[None, :] // GDN_CHUNK) & (r[None, :] <= r[:, None]), BF16)
    lane, col = np.arange(LANES)[:, None], np.arange(nq)[None, :]
    expand_gam = jnp.asarray(lane == col // dk, BF16)
    expand_beta = jnp.asarray(lane == nh + col // dk, BF16)
    nat = jax.ShapeDtypeStruct((b, l, nq), F32)
    nat_spec = pl.BlockSpec((1, tm, nq), lambda bi, i: (bi, i, 0))
    const = lambda a: pl.BlockSpec(a.shape, lambda bi, i: (0, 0))
    halo_blocks = tm // SUBLANES
    return pl.pallas_call(
        functools.partial(_gdn_prep_body, tm=tm),
        grid=(b, l // tm),
        in_specs=[pl.BlockSpec((1, tm, wq), lambda bi, i: (bi, i, COL_GQKV // wq)),
                  pl.BlockSpec((1, SUBLANES, wq),
                               lambda bi, i: (bi, jnp.maximum(i * halo_blocks - 1, 0), COL_GQKV // wq)),
                  pl.BlockSpec((1, tm, LANES), lambda bi, i: (bi, i, COL_SMA // LANES)),
                  const(conv_w), const(alog), const(dtb), const(tri), const(expand_beta), const(expand_gam)],
        out_specs=[nat_spec] * 5 + [pl.BlockSpec((1, SUBLANES, tm), lambda bi, i: (bi, 0, i))],
        out_shape=[nat] * 5 + [jax.ShapeDtypeStruct((b, SUBLANES, l), F32)],
        scratch_shapes=[pltpu.VMEM((tm + SUBLANES, wq), F32)],
        compiler_params=_cparams("parallel", "arbitrary"),
        name="gdn_prep",
    )(proj3, proj3, proj3, conv_w, alog, dtb, tri, expand_beta, expand_gam)


def _split_bf16(a):
    hi = a.astype(BF16)
    lo = (a - hi.astype(F32)).astype(BF16)
    return hi, lo


def _bmm_split_rhs(a, b):
    ah = a.astype(BF16)
    bh, bl = _split_bf16(b)
    f = lambda u, v: jnp.einsum("nij,njk->nik", u, v, preferred_element_type=F32)
    return f(ah, bh) + f(ah, bl)


def _gdn_chunk_body(q_ref, k_ref, v_ref, beta_ref, gam_ref, gr_ref, z_ref, ng_ref, o_ref, s_ref,
                    *, nc):
    c = GDN_CHUNK
    nh, dk, dv = GDN_HEADS, GDN_DK, GDN_DV
    assert c == dk == dv

    @pl.when(pl.program_id(1) == 0)
    def _():
        s_ref[...] = jnp.zeros_like(s_ref)

    nb = nh * nc
    rows = lax.broadcasted_iota(jnp.int32, (nb, c, c), 1)
    cols = lax.broadcasted_iota(jnp.int32, (nb, c, c), 2)
    causal = rows >= cols
    strict = rows > cols
    def heads(ref):
        full = ref[0]
        return jnp.concatenate([full[:, h * dk:(h + 1) * dk].reshape(nc, c, dk) for h in range(nh)], axis=0)

    q, k, v = heads(q_ref), heads(k_ref), heads(v_ref)
    beta = heads(beta_ref)
    gc = heads(gam_ref)
    gr = jnp.concatenate([gr_ref[0, h:h + 1, n * c:(n + 1) * c].reshape(1, 1, c)
                          for h in range(nh) for n in range(nc)], axis=0)
    gr = jnp.broadcast_to(gr, (nb, c, c))
    decay = jnp.exp(jnp.where(causal, gc - gr, NEG_INF))
    kb = k.astype(BF16)
    qk = jnp.einsum("nid,njd->nij", jnp.concatenate([q.astype(BF16), kb], axis=1), kb,
                    preferred_element_type=F32)
    attn = qk[:, :c] * decay
    x = -jnp.where(strict, beta * qk[:, c:] * decay, 0.0)
    eg = jnp.exp(gc)
    uw = jnp.concatenate([beta * v, beta * eg * k], axis=2)
    levels = 6
    assert 2 ** levels == c
    for level in range(levels):
        if level < levels - 1:
            y = _bmm_split_rhs(x, jnp.concatenate([uw, x], axis=2))
            uw = uw + y[:, :, :2 * dv]
            x = y[:, :, 2 * dv:]
        else:
            uw = uw + _bmm_split_rhs(x, uw)
    g_last = gc[:, c - 1:c, :]
    k_dec = (k * jnp.exp(g_last - gc)).astype(BF16)
    uwb = uw.astype(BF16)
    ku = jnp.stack([lax.dot_general(k_dec[n], uwb[n], (((0,), (0,)), ((), ())), preferred_element_type=F32)
                    for n in range(nb)], axis=0)
    au = jnp.einsum("nij,njv->niv", attn.astype(BF16), uwb, preferred_element_type=F32)
    q_eff = (q * eg - au[:, :, dv:]).astype(BF16)
    split = lambda a: a.reshape((nh, nc) + a.shape[1:])
    s_in, s_map = split(ku[:, :, :dv]), split(ku[:, :, dv:].astype(BF16))
    gl = split(jnp.exp(g_last))
    s = s_ref[...]
    states = []
    for n in range(nc):
        states.append(s)
        s = gl[:, n] * s + s_in[:, n] - jnp.einsum("hkj,hjv->hkv", s_map[:, n], s.astype(BF16),
                                                  preferred_element_type=F32)
    s_ref[...] = s
    s_all = jnp.stack(states, axis=1).reshape(nb, dk, dv).astype(BF16)
    o = jnp.einsum("nik,nkv->niv", q_eff, s_all, preferred_element_type=F32) + au[:, :, :dv]
    o = split(o * lax.rsqrt(jnp.mean(o * o, axis=-1, keepdims=True) + EPS) * ng_ref[...])
    for n in range(nc):
        o_all = jnp.concatenate([o[h, n] for h in range(nh)], axis=1)
        o_ref[0, n * c:(n + 1) * c, :] = (o_all * _silu(z_ref[0, n * c:(n + 1) * c, :])).astype(o_ref.dtype)


def gdn_chunk(q, k, v, beta, gam, gr, proj3, norm_g, lt):
    b, l, wz = q.shape
    nh = GDN_HEADS
    nc = lt // GDN_CHUNK
    hm_spec = pl.BlockSpec((1, lt, wz), lambda bi, i: (bi, i, 0))
    return pl.pallas_call(
        functools.partial(_gdn_chunk_body, nc=nc),
        grid=(b, l // lt),
        in_specs=[hm_spec] * 5 + [
            pl.BlockSpec((1, SUBLANES, lt), lambda bi, i: (bi, 0, i)),
            pl.BlockSpec((1, lt, wz), lambda bi, i: (bi, i, COL_AZ // wz)),
            pl.BlockSpec((1, GDN_DV), lambda bi, i: (0, 0))],
        out_specs=pl.BlockSpec((1, lt, wz), lambda bi, i: (bi, i, 0)),
        out_shape=jax.ShapeDtypeStruct((b, l, wz), BF16),
        scratch_shapes=[pltpu.VMEM((nh, GDN_DK, GDN_DV), F32)],
        compiler_params=_cparams("parallel", "arbitrary"),
        name="gdn_chunk",
    )(q, k, v, beta, gam, gr, proj3, norm_g.reshape(1, GDN_DV))


MASK_BIAS = -1e30
LOG2E = 1.4426950408889634


def _moba_gate_body(q_ref, kall_ref, k_ref, v_ref, pq_ref, pb_ref, qa_ref, ka_ref, vb_ref, km_ref, *, nb):
    i = pl.program_id(1)
    nh, hd = MOBA_HEADS, MOBA_HD
    w = nh * hd

    assert nb & (nb - 1) == 0 and hd & (hd - 1) == 0
    nb_bits, hd_bits = nb.bit_length() - 1, hd.bit_length() - 1

    @pl.when(i == 0)
    def _():
        kmean = jnp.mean(kall_ref[0].reshape(nb, MOBA_BLOCK, w), axis=1)
        lane = lax.broadcasted_iota(jnp.int32, (nb, w), 1)
        for h in range(nh):
            km_ref[h * nb:(h + 1) * nb, :] = jnp.where(jnp.right_shift(lane, hd_bits) == h, kmean, 0.0)

    q = q_ref[0]
    gate = lax.dot_general(km_ref[...], q, (((1,), (1,)), ((), ())),
                           precision=HIGHEST, preferred_element_type=F32)
    blk = lax.broadcasted_iota(jnp.int32, (nb, gate.shape[1]), 0)
    valid = blk < i
    big = jnp.int32(nb)
    masked = []
    for h in range(nh):
        g_h = gate[h * nb:(h + 1) * nb, :]
        avail = valid
        for _ in range(MOBA_TOPK):
            m = jnp.max(jnp.where(avail, g_h, NEG_INF), axis=0, keepdims=True)
            cand = avail & (g_h == m)
            first = jnp.min(jnp.where(cand, blk, big), axis=0, keepdims=True)
            avail = avail & (blk != first)
        masked.append(avail.astype(F32))
    masked = jnp.concatenate(masked, axis=0).astype(BF16)
    bias = lax.dot_general(masked, pb_ref[...], (((0,), (0,)), ((), ())), preferred_element_type=F32)
    q_tiles = _dot(q * (hd ** -0.5 * LOG2E), pq_ref[...]) + bias * MASK_BIAS
    qa_ref[0] = q_tiles.astype(BF16)
    tile_lane = jnp.bitwise_and(lax.broadcasted_iota(jnp.int32, q_tiles.shape, 1), LANES - 1)
    ka_ref[0] = (_dot(k_ref[0], pq_ref[...]) + (tile_lane == hd + i).astype(F32)).astype(BF16)
    vb_ref[0] = v_ref[0].astype(BF16)


def moba_gate(proj3):
    b, l, _ = proj3.shape
    nh, hd = MOBA_HEADS, MOBA_HD
    w = nh * hd
    nb = l // MOBA_BLOCK
    tq = MOBA_BLOCK
    assert hd + nb <= LANES
    src = np.arange(w)
    place_q = np.zeros((w, nh * LANES), np.float32)
    place_q[src, (src // hd) * LANES + src % hd] = 1.0
    srcb = np.arange(nh * nb)
    place_b = np.zeros((nh * nb, nh * LANES), np.float32)
    place_b[srcb, (srcb // nb) * LANES + hd + srcb % nb] = 1.0
    place_q, place_b = jnp.asarray(place_q, BF16), jnp.asarray(place_b, BF16)
    cq = COL_MQKV // w
    row = lambda c: pl.BlockSpec((1, tq, w), lambda bi, i: (bi, i, c))
    full = lambda a: pl.BlockSpec(a.shape, lambda bi, i: (0, 0))
    tiles = jax.ShapeDtypeStruct((b, l, nh * LANES), BF16)
    tspec = pl.BlockSpec((1, tq, nh * LANES), lambda bi, i: (bi, i, 0))
    return pl.pallas_call(
        functools.partial(_moba_gate_body, nb=nb),
        grid=(b, l // tq),
        in_specs=[row(cq), pl.BlockSpec((1, l, w), lambda bi, i: (bi, 0, cq + 1)), row(cq + 1), row(cq + 2),
                  full(place_q), full(place_b)],
        out_specs=[tspec, tspec, pl.BlockSpec((1, tq, w), lambda bi, i: (bi, i, 0))],
        out_shape=[tiles, tiles, jax.ShapeDtypeStruct((b, l, w), BF16)],
        scratch_shapes=[pltpu.VMEM((nh * nb, w), F32)],
        compiler_params=_cparams("parallel", "arbitrary"),
        name="moba_gate",
    )(proj3, proj3, proj3, proj3, place_q, place_b)


M_INIT = -2e38


def _flash_body(q_ref, k_ref, v_ref, o_ref, m_scr, acc_scr, *, nh, tq, tk):
    i = pl.program_id(1)
    row = lax.broadcasted_iota(jnp.int32, (tq, tk), 0)
    col = lax.broadcasted_iota(jnp.int32, (tq, tk), 1)
    ones = jnp.ones((tk, LANES), BF16)
    m_scr[...] = jnp.full(m_scr.shape, M_INIT, F32)
    acc_scr[...] = jnp.zeros(acc_scr.shape, F32)

    def tiles(kv_r0, diag):
        for h in range(nh):
            q = q_ref[0, :, h * LANES:(h + 1) * LANES]
            kt = k_ref[0, pl.ds(kv_r0, tk), h * LANES:(h + 1) * LANES]
            v_lo = (h // 2) * LANES
            vt = jnp.concatenate([v_ref[0, pl.ds(kv_r0, tk), v_lo:v_lo + LANES], ones], axis=1)
            s = _dot_nt(q, kt)
            if diag is not None:
                s = jnp.where(col + diag * tk <= row, s, MASK_BIAS)
            m_prev = m_scr[h]
            m_next = jnp.maximum(m_prev, jnp.max(s, axis=1, keepdims=True))
            p = jnp.exp2(s - jnp.concatenate([m_next] * (tk // LANES), axis=1))
            alpha = jnp.exp2(m_prev - m_next)
            acc_scr[h] = jnp.concatenate([alpha, alpha], axis=1) * acc_scr[h] + _dot(p, vt)
            m_scr[h] = m_next

    def past(j, carry):
        tiles(pl.multiple_of(j * tk, tk), None)
        return carry

    lax.fori_loop(0, i * (tq // tk), past, 0)
    for d in range(tq // tk):
        tiles(pl.multiple_of(i * tq + d * tk, tk), d)
    lane = lax.broadcasted_iota(jnp.int32, (tq, LANES), 1)
    low = lane < LANES // 2
    outs = [acc_scr[h][:, :LANES] / acc_scr[h][:, LANES:] for h in range(nh)]
    pairs = [jnp.where(low, outs[2 * p], outs[2 * p + 1]) for p in range(nh // 2)]
    o_ref[0] = jnp.concatenate(pairs, axis=1).astype(o_ref.dtype)


def flash_attention(q, k, v, *, nh, name):
    b, l, wq = q.shape
    wv = v.shape[2]
    tq = tk = 2 * MOBA_BLOCK
    return pl.pallas_call(
        functools.partial(_flash_body, nh=nh, tq=tq, tk=tk),
        grid=(b, l // tq),
        in_specs=[pl.BlockSpec((1, tq, wq), lambda bi, i: (bi, i, 0)),
                  pl.BlockSpec((1, l, wq), lambda bi, i: (bi, 0, 0)),
                  pl.BlockSpec((1, l, wv), lambda bi, i: (bi, 0, 0))],
        out_specs=pl.BlockSpec((1, tq, wv), lambda bi, i: (bi, i, 0)),
        out_shape=jax.ShapeDtypeStruct((b, l, wv), BF16),
        scratch_shapes=[pltpu.VMEM((nh, tq, LANES), F32), pltpu.VMEM((nh, tq, 2 * LANES), F32)],
        compiler_params=_cparams("parallel", "arbitrary"),
        name=name,
    )(q, k, v)


def _mla_proj_body(cq_ref, ckv_ref, sma_ref, smb_ref, qg_ref, kvg_ref, wq_ref, wkv_ref,
                   cq_t, sq_t, ck_t, sk_t, q_ref, k_ref, v_ref):
    nh = MLA_HEADS
    cq = cq_ref[0]
    hq = cq * lax.rsqrt(jnp.sum(cq * cq, axis=-1, keepdims=True) * (1.0 / MLA_Q_LORA) + EPS) * qg_ref[...]
    qq = _dot(hq, wq_ref[...])
    half = nh * LANES
    q_ref[0] = (qq[:, :half] * cq_t[...] + qq[:, half:] * sq_t[...]).astype(BF16)
    ckv = ckv_ref[0]
    hkv = ckv * lax.rsqrt(jnp.mean(ckv * ckv, axis=-1, keepdims=True) + EPS) * kvg_ref[...]
    kv = _dot(hkv, wkv_ref[...])
    krot = sma_ref[0] * ck_t[...] + smb_ref[0] * sk_t[...]
    k_ref[0] = (kv[:, :half] + jnp.concatenate([krot] * nh, axis=1)).astype(BF16)
    v_ref[0] = kv[:, half:].astype(BF16)


def mla_weights(q_norm_g, w_uq, kv_norm_g, w_ukv):
    nh, dn, dr, dvh = MLA_HEADS, MLA_NOPE, MLA_ROPE, MLA_V
    hr = dr // 2
    cqw = COL_CKV - COL_CQ
    wq3 = w_uq.reshape(MLA_Q_LORA, nh, dn + dr).astype(BF16)
    zeros = lambda n: jnp.zeros((MLA_Q_LORA, nh, n), BF16)
    w1 = jnp.concatenate([wq3, zeros(LANES - dn - dr)], axis=2)
    w2 = jnp.concatenate([zeros(dn), -wq3[:, :, dn + hr:], wq3[:, :, dn:dn + hr], zeros(LANES - dn - dr)], axis=2)
    wq = jnp.concatenate([w1.reshape(MLA_Q_LORA, nh * LANES), w2.reshape(MLA_Q_LORA, nh * LANES)], axis=1)
    wq = jnp.pad(wq, ((0, cqw - MLA_Q_LORA), (0, 0)))
    qg = jnp.pad(q_norm_g, (0, cqw - MLA_Q_LORA)).reshape(1, cqw)
    wkv3 = w_ukv.reshape(MLA_KV_LORA, nh, dn + dvh)
    wk = jnp.pad(wkv3[:, :, :dn], ((0, 0), (0, 0), (0, LANES - dn)))
    wkv = jnp.concatenate([wk.reshape(MLA_KV_LORA, nh * LANES),
                           wkv3[:, :, dn:].reshape(MLA_KV_LORA, nh * dvh)], axis=1).astype(BF16)
    return qg, kv_norm_g.reshape(1, MLA_KV_LORA), wq, wkv


def mla_tables(cos, sin):
    l = cos.shape[0]
    nh, dn, dr = MLA_HEADS, MLA_NOPE, MLA_ROPE
    scale = (dn + dr) ** -0.5 * LOG2E
    cc = jnp.concatenate([cos, cos], axis=1)
    ss = jnp.concatenate([sin, sin], axis=1)
    pad = jnp.zeros((l, LANES - dn - dr), F32)
    cq_tile = jnp.concatenate([jnp.ones((l, dn), F32), cc, pad], axis=1) * scale
    sq_tile = jnp.concatenate([jnp.zeros((l, dn), F32), ss, pad], axis=1) * scale
    cq_t = jnp.tile(cq_tile, (1, nh))
    sq_t = jnp.tile(sq_tile, (1, nh))
    ck_t = jnp.concatenate([jnp.zeros((l, dn), F32), cc, pad], axis=1)
    sk_t = jnp.concatenate([jnp.zeros((l, dn), F32), ss, pad], axis=1)
    return cq_t, sq_t, ck_t, sk_t


def mla_proj(proj3, weights, tables, tm):
    b, l, _ = proj3.shape
    nh, dvh = MLA_HEADS, MLA_V
    cqw = COL_CKV - COL_CQ
    qg, kvg, wq, wkv = weights
    cq_t, sq_t, ck_t, sk_t = tables
    row = lambda w_, c_: pl.BlockSpec((1, tm, w_), lambda i, bi: (bi, i, c_ // w_))
    full = lambda a: pl.BlockSpec(a.shape, lambda i, bi: (0, 0))
    tab = lambda w_: pl.BlockSpec((tm, w_), lambda i, bi: (i, 0))
    out_q = jax.ShapeDtypeStruct((b, l, nh * LANES), BF16)
    out_v = jax.ShapeDtypeStruct((b, l, nh * dvh), BF16)
    return pl.pallas_call(
        _mla_proj_body,
        grid=(l // tm, b),
        in_specs=[row(cqw, COL_CQ), row(MLA_KV_LORA, COL_CKV), row(LANES, COL_SMA), row(LANES, COL_SMB),
                  full(qg), full(kvg), full(wq), full(wkv),
                  tab(nh * LANES), tab(nh * LANES), tab(LANES), tab(LANES)],
        out_specs=[pl.BlockSpec((1, tm, nh * LANES), lambda i, bi: (bi, i, 0)),
                   pl.BlockSpec((1, tm, nh * LANES), lambda i, bi: (bi, i, 0)),
                   pl.BlockSpec((1, tm, nh * dvh), lambda i, bi: (bi, i, 0))],
        out_shape=[out_q, out_q, out_v],
        compiler_params=_cparams("parallel", "parallel"),
        name="mla_proj",
    )(proj3, proj3, proj3, proj3, qg, kvg, wq, wkv, cq_t, sq_t, ck_t, sk_t)


def _s5_state_body(u_ref, pst_ref, lam_ref, xp_ref, pu_scr, xp_scr, *, nch, nseq):
    half = S5_GROUPS * S5_STATE
    pu_scr[...] = _dot(u_ref[...], pst_ref[...])
    ar = lam_ref[0:1, :]
    ai = lam_ref[1:2, :]

    def step(kc, carry):
        nxt = []
        for sq in range(nseq):
            xr, xi = carry[sq]
            r = sq * nch + kc
            xp_scr[pl.ds(r, 1), :] = jnp.concatenate([xr, xi], axis=1)
            e = pu_scr[pl.ds(r, 1), :]
            nxt.append((ar * xr - ai * xi + e[:, :half], ar * xi + ai * xr + e[:, half:]))
        return tuple(nxt)

    zero = jnp.zeros((1, half), F32)
    lax.fori_loop(0, nch, step, tuple((zero, zero) for _ in range(nseq)))
    xp_ref[...] = xp_scr[...].astype(BF16)


def _s5_out_body(u_ref, xp_ref, kr_ref, qst_ref, y_ref):
    cn, w = S5_CHUNK, S5_WIDTH
    xp = xp_ref[...]
    for t in range(cn):
        y_ref[:, t * w:(t + 1) * w] = (_dot(u_ref[:, :(t + 1) * w], kr_ref[(cn - 1 - t) * w:, :])
                                       + _dot(xp, qst_ref[:, t * w:(t + 1) * w]))


def s5_weights(lam_re, lam_im, b_re, b_im, c_re, c_im, d, log_dt):
    cn = S5_CHUNK
    g, p, cg = S5_GROUPS, S5_STATE, S5_GROUP
    ein = functools.partial(jnp.einsum, precision=HIGHEST)
    eye = jnp.eye(g, dtype=F32)
    lr = jnp.minimum(lam_re, -1e-4)
    li = lam_im
    dt = jnp.exp(log_dt)[:, None]
    mag = jnp.exp(lr * dt)
    ar, ai = mag * jnp.cos(li * dt), mag * jnp.sin(li * dt)
    den = lr * lr + li * li
    cr = ((ar - 1.0) * lr + ai * li) / den
    ci = (ai * lr - (ar - 1.0) * li) / den
    bbr = cr[..., None] * b_re - ci[..., None] * b_im
    bbi = cr[..., None] * b_im + ci[..., None] * b_re
    def powers(taus):
        tau = jnp.asarray(taus, F32)[None, :, None]
        pmag = jnp.exp((lr * dt)[:, None, :] * tau)
        pang = (li * dt)[:, None, :] * tau
        return pmag * jnp.cos(pang), pmag * jnp.sin(pang)

    def c_times(pwr, pwi):
        return (c_re[:, None] * pwr[:, :, None, :] - c_im[:, None] * pwi[:, :, None, :],
                c_re[:, None] * pwi[:, :, None, :] + c_im[:, None] * pwr[:, :, None, :])

    dsc_r, dsc_i = powers(np.arange(cn - 1, -1, -1))
    asc_r, asc_i = powers(np.arange(1, cn + 1))
    clr, cli = c_times(dsc_r, dsc_i)
    kt = ein("gtap,gpc->gtac", clr, bbr) - ein("gtap,gpc->gtac", cli, bbi)
    is_lag0 = (jnp.arange(cn) == cn - 1).astype(F32)[None, :, None, None]
    kt = kt + is_lag0 * (d[:, :, None] * jnp.eye(cg, dtype=F32)[None])[:, None]
    kr = kt.transpose(1, 0, 3, 2)[:, :, :, None, :] * eye[None, :, None, :, None]
    kr = kr.astype(BF16).reshape(cn * g * cg, g * cg)
    bbr_t, bbi_t = bbr.transpose(0, 2, 1)[:, None], bbi.transpose(0, 2, 1)[:, None]
    pr = dsc_r[:, :, None, :] * bbr_t - dsc_i[:, :, None, :] * bbi_t
    pi = dsc_r[:, :, None, :] * bbi_t + dsc_i[:, :, None, :] * bbr_t
    p2 = jnp.stack([pr, pi], axis=3).transpose(1, 0, 2, 3, 4)
    pst = p2[:, :, :, :, None, :] * eye[None, :, None, None, :, None]
    pst = pst.astype(BF16).reshape(cn * g * cg, 2 * g * p)
    clr, cli = c_times(asc_r, asc_i)
    q2 = jnp.stack([clr, -cli], axis=0).transpose(0, 1, 4, 2, 3)
    qst = q2[:, :, :, :, None, :] * eye[None, :, None, None, :, None]
    qst = qst.astype(BF16).reshape(2 * g * p, cn * g * cg)
    lam = jnp.stack([asc_r[:, cn - 1].reshape(g * p), asc_i[:, cn - 1].reshape(g * p)], axis=0)
    return kr, pst, qst, lam


def s5_scan(su, weights, batch):
    t, w = su.shape
    cn = S5_CHUNK
    kr, pst, qst, lam = weights
    nr = t // cn
    nch = nr // batch
    nseq = 2
    rows = nseq * nch
    u = su.reshape(nr, cn * w)
    ns = pst.shape[1]
    once = lambda a: pl.BlockSpec(a.shape, lambda i: (0, 0), pipeline_mode=pl.Buffered(1))
    rowb = lambda width: pl.BlockSpec((rows, width), lambda i: (i, 0))
    xp = pl.pallas_call(
        functools.partial(_s5_state_body, nch=nch, nseq=nseq),
        grid=(nr // rows,),
        in_specs=[rowb(cn * w), once(pst), once(lam)],
        out_specs=rowb(ns),
        out_shape=jax.ShapeDtypeStruct((nr, ns), BF16),
        scratch_shapes=[pltpu.VMEM((rows, ns), F32)] * 2,
        compiler_params=_cparams("parallel"),
        name="s5_state",
    )(u, pst, lam)
    y = pl.pallas_call(
        _s5_out_body,
        grid=(nr // rows,),
        in_specs=[rowb(cn * w), rowb(ns), once(kr), once(qst)],
        out_specs=rowb(cn * w),
        out_shape=jax.ShapeDtypeStruct((nr, cn * w), F32),
        compiler_params=_cparams("parallel"),
        name="s5_out",
    )(u, xp, kr, qst)
    return y.reshape(t, w)


def _out_proj_body(x_ref, oa_ref, ob_ref, oc_ref, ys_ref, gw_ref, gb_ref, wo_ref, o_ref):
    y = jax.nn.gelu(ys_ref[...])
    od = y * _sigmoid(_dot(y, gw_ref[...]) + gb_ref[...])
    mix = jnp.concatenate([oa_ref[...].astype(BF16), ob_ref[...], oc_ref[...], od.astype(BF16)], axis=1)
    o_ref[...] = x_ref[...] + _dot(mix, wo_ref[...])


def out_proj(x, oa, ob, oc, ys, glu_w, glu_b, w_out, tm):
    t, dm = x.shape
    wm = oa.shape[1]
    row = lambda w_: pl.BlockSpec((tm, w_), lambda i: (i, 0))
    full = lambda a: pl.BlockSpec(a.shape, lambda i: (0, 0))
    gb = glu_b.reshape(1, wm)
    return pl.pallas_call(
        _out_proj_body,
        grid=(t // tm,),
        in_specs=[row(dm), row(wm), row(wm), row(wm), row(wm), full(glu_w), full(gb), full(w_out)],
        out_specs=row(dm),
        out_shape=jax.ShapeDtypeStruct((t, dm), F32),
        compiler_params=_cparams("parallel"),
        name="out_proj",
    )(x, oa, ob, oc, ys, glu_w, gb, w_out)


def _ffn_body(x_ref, halo_ref, g_ref, wup_ref, cw_ref, wdn_ref, fg_ref, o_ref, up_scr, *, tm, fc, final_norm):
    i = pl.program_id(1)
    keep = (i > 0).astype(F32)

    def norm(v, g):
        return v * lax.rsqrt(jnp.mean(v * v, axis=-1, keepdims=True) + EPS) * g

    x = x_ref[0]
    h_main = norm(x, g_ref[...]).astype(BF16)
    h_halo = norm(halo_ref[0], g_ref[...]).astype(BF16)
    acc = x
    for c in range(D_FF // fc):
        acts = []
        for part in range(2):
            lo = part * D_FF + c * fc
            up_scr[0:SUBLANES, :] = _dot(h_halo, wup_ref[:, lo:lo + fc]) * keep
            up_scr[SUBLANES:, :] = _dot(h_main, wup_ref[:, lo:lo + fc])
            w = cw_ref[:, lo:lo + fc]
            y = w[FFN_CONV - 1:FFN_CONV, :] * up_scr[pl.ds(SUBLANES, tm), :]
            for tap in range(FFN_CONV - 1):
                y = y + w[tap:tap + 1, :] * up_scr[pl.ds(SUBLANES - (FFN_CONV - 1) + tap, tm), :]
            acts.append(y)
        act = _silu(acts[0]) * acts[1]
        acc = acc + _dot(act, wdn_ref[c * fc:(c + 1) * fc, :])
    o_ref[0] = norm(acc, fg_ref[...]) if final_norm else acc


def ffn(x3, norm_g, w_up, conv_w, w_down, final_g, tm, fc):
    b, l, dm = x3.shape
    halo_blocks = tm // SUBLANES
    once = lambda a: pl.BlockSpec(a.shape, lambda bi, i: (0, 0), pipeline_mode=pl.Buffered(1))
    g = norm_g.reshape(1, dm)
    final_norm = final_g is not None
    fg = final_g.reshape(1, dm) if final_norm else g
    return pl.pallas_call(
        functools.partial(_ffn_body, tm=tm, fc=fc, final_norm=final_norm),
        grid=(b, l // tm),
        in_specs=[pl.BlockSpec((1, tm, dm), lambda bi, i: (bi, i, 0)),
                  pl.BlockSpec((1, SUBLANES, dm), lambda bi, i: (bi, jnp.maximum(i * halo_blocks - 1, 0), 0)),
                  once(g), once(w_up), once(conv_w), once(w_down), once(fg)],
        out_specs=pl.BlockSpec((1, tm, dm), lambda bi, i: (bi, i, 0)),
        out_shape=jax.ShapeDtypeStruct((b, l, dm), F32),
        scratch_shapes=[pltpu.VMEM((tm + SUBLANES, fc), F32)],
        compiler_params=_cparams("parallel", "arbitrary"),
        name="ffn",
    )(x3, x3, g, w_up, conv_w, w_down, fg)


def _permute_w_in(w_in):
    dm = w_in.shape[0]
    gw, mw = GDN_HEADS * GDN_DK, MOBA_HEADS * MOBA_HD
    o_aq, o_az, o_aa, o_ab = 0, 3 * gw, 4 * gw, 4 * gw + GDN_HEADS
    o_mq = o_ab + GDN_HEADS
    o_cq = o_mq + 3 * mw
    o_ckv = o_cq + MLA_Q_LORA
    o_ckr = o_ckv + MLA_KV_LORA
    o_su = o_ckr + MLA_ROPE
    hr = MLA_ROPE // 2
    z = lambda n: jnp.zeros((dm, n), w_in.dtype)
    ckr = w_in[:, o_ckr:o_ckr + MLA_ROPE]
    ckr_swap = jnp.concatenate([-ckr[:, hr:], ckr[:, :hr]], axis=1)
    cols = [w_in[:, o_aq:o_aq + 3 * gw],
            w_in[:, o_mq:o_mq + 3 * mw],
            w_in[:, o_az:o_az + gw],
            w_in[:, o_su:o_su + S5_WIDTH],
            w_in[:, o_cq:o_cq + MLA_Q_LORA], z(COL_CKV - COL_CQ - MLA_Q_LORA),
            w_in[:, o_ckv:o_ckv + MLA_KV_LORA],
            w_in[:, o_aa:o_aa + 2 * GDN_HEADS], z(MLA_NOPE - 2 * GDN_HEADS), ckr, z(LANES - MLA_NOPE - MLA_ROPE),
            z(MLA_NOPE), ckr_swap, z(LANES - MLA_NOPE - MLA_ROPE)]
    return jnp.concatenate(cols, axis=1).astype(BF16)


def _layer(x3, tables, w, final_g):
    b, l, dm = x3.shape
    t = b * l
    x2 = x3.reshape(t, dm)
    proj, su = in_proj(x2, w["norm1_g"], w["w_in"], ROW_TILE)
    proj3 = proj.reshape(b, l, N_PROJ)

    q, k, v, beta, gam, gr = gdn_prep(proj3, w["gdn_conv_w"], w["gdn_alog"], w["gdn_dtb"], ROW_TILE)
    o_a = gdn_chunk(q, k, v, beta, gam, gr, proj3, w["gdn_norm_g"], ROW_TILE).reshape(t, GDN_HEADS * GDN_DV)

    mq, mk, mv = moba_gate(proj3)
    o_b = flash_attention(mq, mk, mv, nh=MOBA_HEADS, name="flash_moba").reshape(t, MOBA_HEADS * MOBA_HD)

    cq, ck, cv = mla_proj(proj3, w["mla"], tables, ROW_TILE)
    o_c = flash_attention(cq, ck, cv, nh=MLA_HEADS, name="flash_mla").reshape(t, MLA_HEADS * MLA_V)

    ys = s5_scan(su, w["s5"], b)

    x2 = out_proj(x2, o_a, o_b, o_c, ys, w["s5_glu_w"], w["s5_glu_b"], w["w_out"], ROW_TILE)
    return ffn(x2.reshape(b, l, dm), w["norm2_g"], w["ffn_w_up"], w["ffn_conv_w"], w["ffn_w_down"], final_g,
               FFN_ROW_TILE, D_FF // 2)


def _prepare_weights(p):
    nh = GDN_HEADS
    return {
        "norm1_g": p["norm1_g"], "w_in": _permute_w_in(p["w_in"]),
        "gdn_conv_w": p["gdn_conv_w"], "gdn_norm_g": p["gdn_norm_g"],
        "gdn_alog": jnp.pad(p["gdn_a_log"], (0, LANES - nh)).reshape(1, LANES),
        "gdn_dtb": jnp.pad(p["gdn_dt_bias"], (0, LANES - nh)).reshape(1, LANES),
        "mla": mla_weights(p["mla_q_norm_g"], p["mla_w_uq"], p["mla_kv_norm_g"], p["mla_w_ukv"]),
        "s5": s5_weights(p["s5_lam_re"], p["s5_lam_im"], p["s5_b_re"], p["s5_b_im"], p["s5_c_re"], p["s5_c_im"],
                         p["s5_d"], p["s5_log_dt"]),
        "s5_glu_w": p["s5_glu_w"].astype(BF16), "s5_glu_b": p["s5_glu_b"], "w_out": p["w_out"].astype(BF16),
        "norm2_g": p["norm2_g"], "ffn_w_up": p["ffn_w_up"].astype(BF16), "ffn_conv_w": p["ffn_conv_w"],
        "ffn_w_down": p["ffn_w_down"].astype(BF16),
    }


_LAYER_KEYS = ("norm1_g", "w_in", "gdn_conv_w", "gdn_a_log", "gdn_dt_bias", "gdn_norm_g",
               "mla_q_norm_g", "mla_w_uq", "mla_kv_norm_g", "mla_w_ukv",
               "s5_lam_re", "s5_lam_im", "s5_b_re", "s5_b_im", "s5_c_re", "s5_c_im", "s5_d", "s5_log_dt",
               "s5_glu_w", "s5_glu_b", "w_out", "norm2_g", "ffn_w_up", "ffn_conv_w", "ffn_w_down")


def kernel(x, norm1_g, w_in, gdn_conv_w, gdn_a_log, gdn_dt_bias, gdn_norm_g, mla_q_norm_g, mla_w_uq,
           mla_kv_norm_g, mla_w_ukv, s5_lam_re, s5_lam_im, s5_b_re, s5_b_im, s5_c_re, s5_c_im, s5_d,
           s5_log_dt, s5_glu_w, s5_glu_b, w_out, norm2_g, ffn_w_up, ffn_conv_w, ffn_w_down, final_norm_g):
    stacked = (norm1_g, w_in, gdn_conv_w, gdn_a_log, gdn_dt_bias, gdn_norm_g, mla_q_norm_g, mla_w_uq,
               mla_kv_norm_g, mla_w_ukv, s5_lam_re, s5_lam_im, s5_b_re, s5_b_im, s5_c_re, s5_c_im, s5_d,
               s5_log_dt, s5_glu_w, s5_glu_b, w_out, norm2_g, ffn_w_up, ffn_conv_w, ffn_w_down)
    l = x.shape[1]
    inv_freq = ROPE_THETA ** (-jnp.arange(0, MLA_ROPE, 2, dtype=F32) / MLA_ROPE)
    ang = jnp.arange(l, dtype=F32)[:, None] * inv_freq[None, :]
    tables = mla_tables(jnp.cos(ang), jnp.sin(ang))
    weights = jax.vmap(_prepare_weights)(dict(zip(_LAYER_KEYS, stacked)))
    depth = norm1_g.shape[0]
    for i in range(depth):
        w = jax.tree_util.tree_map(lambda a: a[i], weights)
        x = _layer(x, tables, w, final_norm_g if i == depth - 1 else None)
    return x
```

```python
import functools
import math

import jax
import jax.numpy as jnp
import numpy as np
from jax import lax
from jax.experimental import pallas as pl
from jax.experimental.pallas import tpu as pltpu

F32 = jnp.float32
BF16 = jnp.bfloat16
NEG_INF = float("-inf")
EPS = 1e-6
HIGHEST = lax.Precision.HIGHEST

D_MODEL = 1024
GDN_HEADS, GDN_DK, GDN_DV, GDN_CONV, GDN_CHUNK = 4, 64, 64, 4, 64
MOBA_HEADS, MOBA_HD, MOBA_BLOCK, MOBA_TOPK = 4, 64, 256, 3
MLA_HEADS, MLA_Q_LORA, MLA_KV_LORA, MLA_NOPE, MLA_ROPE, MLA_V = 4, 192, 128, 64, 32, 64
ROPE_THETA = 10000.0
S5_GROUP, S5_GROUPS, S5_STATE = 16, 16, 64
S5_WIDTH = S5_GROUP * S5_GROUPS
D_FF, FFN_CONV = 2816, 3

LANES = 128
SUBLANES = 8
VMEM_LIMIT = 60 * 1024 * 1024

COL_GQKV, COL_MQKV, COL_AZ, COL_SU, COL_CQ, COL_CKV, COL_SMA, COL_SMB = (
    0, 768, 1536, 1792, 2048, 2304, 2432, 2560)
N_PROJ = 2688
S5_CHUNK = 16
ROW_TILE = 512
FFN_ROW_TILE = 1024


def _cparams(*sems):
    return pltpu.CompilerParams(dimension_semantics=sems, vmem_limit_bytes=VMEM_LIMIT)


def _dot(a, b):
    return jnp.dot(a.astype(BF16), b.astype(BF16), preferred_element_type=F32)


def _dot_nt(a, b):
    return lax.dot_general(a.astype(BF16), b.astype(BF16), (((1,), (1,)), ((), ())),
                           preferred_element_type=F32)


def _split3_bf16(x):
    hi = x.astype(BF16)
    r1 = x - hi.astype(F32)
    mid = r1.astype(BF16)
    return hi, mid, (r1 - mid.astype(F32)).astype(BF16)


def _sigmoid(x):
    return jax.nn.sigmoid(x)


def _silu(x):
    return x * jax.nn.sigmoid(x)


def _in_proj_body(x_ref, g_ref, w_ref, o_ref, su_ref, *, inv_dim):
    x = x_ref[...]
    ms = jnp.sum(x * x, axis=-1, keepdims=True) * inv_dim
    h = x * lax.rsqrt(ms + EPS) * g_ref[...]
    proj = _dot(h, w_ref[...])
    o_ref[...] = proj
    su_ref[...] = proj[:, COL_SU:COL_SU + S5_WIDTH].astype(BF16)


def in_proj(x, g, w, tm):
    t, k = x.shape
    n = w.shape[1]
    return pl.pallas_call(
        functools.partial(_in_proj_body, inv_dim=1.0 / k),
        grid=(t // tm,),
        in_specs=[pl.BlockSpec((tm, k), lambda i: (i, 0)),
                  pl.BlockSpec((1, k), lambda i: (0, 0)),
                  pl.BlockSpec((k, n), lambda i: (0, 0))],
        out_specs=[pl.BlockSpec((tm, n), lambda i: (i, 0)), pl.BlockSpec((tm, S5_WIDTH), lambda i: (i, 0))],
        out_shape=[jax.ShapeDtypeStruct((t, n), F32), jax.ShapeDtypeStruct((t, S5_WIDTH), BF16)],
        compiler_params=_cparams("parallel"),
        name="in_proj",
    )(x, g.reshape(1, k), w)


def _gdn_prep_body(qkv_ref, halo_ref, sm_ref, cw_ref, alog_ref, dtb_ref, tri_ref, eb_ref, eg_ref,
                   q_ref, k_ref, v_ref, beta_ref, gam_ref, gr_ref, xs_ref, *, tm):
    i = pl.program_id(1)
    keep = (i > 0).astype(F32)
    xs_ref[0:SUBLANES, :] = halo_ref[0] * keep
    xs_ref[SUBLANES:, :] = qkv_ref[0]
    w = cw_ref[...]
    nq = GDN_HEADS * GDN_DK
    dk = GDN_DK
    y = w[3:4, :] * xs_ref[pl.ds(SUBLANES, tm), :]
    for tap in range(GDN_CONV - 1):
        y = y + w[tap:tap + 1, :] * xs_ref[pl.ds(SUBLANES - (GDN_CONV - 1) + tap, tm), :]
    y = _silu(y)
    for h in range(GDN_HEADS):
        lanes = slice(h * dk, (h + 1) * dk)
        qh = y[:, h * dk:(h + 1) * dk]
        kh = y[:, nq + h * dk:nq + (h + 1) * dk]
        q_ref[0, :, lanes] = qh * lax.rsqrt(jnp.sum(qh * qh, axis=-1, keepdims=True) + EPS) * (dk ** -0.5)
        k_ref[0, :, lanes] = kh * lax.rsqrt(jnp.sum(kh * kh, axis=-1, keepdims=True) + EPS)
    v_ref[0] = y[:, 2 * nq:]
    sm = sm_ref[0]
    g = -jnp.exp(alog_ref[...]) * jax.nn.softplus(sm + dtb_ref[...])
    beta = _sigmoid(sm)
    tri = tri_ref[...]
    gam = sum(jnp.dot(tri, part, preferred_element_type=F32) for part in _split3_bf16(g))
    pick = (lax.broadcasted_iota(jnp.int32, (SUBLANES, LANES), 0)
            == lax.broadcasted_iota(jnp.int32, (SUBLANES, LANES), 1)).astype(BF16)
    gr_ref[0] = sum(lax.dot_general(pick, part, (((1,), (1,)), ((), ())), preferred_element_type=F32)
                    for part in _split3_bf16(gam))
    beta_ref[0] = sum(jnp.dot(part, eb_ref[...], preferred_element_type=F32) for part in _split3_bf16(beta))
    gam_ref[0] = sum(jnp.dot(part, eg_ref[...], preferred_element_type=F32) for part in _split3_bf16(gam))


def gdn_prep(proj3, conv_w, alog, dtb, tm):
    b, l, _ = proj3.shape
    nh, dk = GDN_HEADS, GDN_DK
    nq = nh * dk
    wq = 3 * nq
    r = np.arange(tm)
    tri = jnp.asarray((r[:, None] // GDN_CHUNK == r[None, :] // GDN_CHUNK) & (r[None, :] <= r[:, None]), BF16)
    lane, col = np.arange(LANES)[:, None], np.arange(nq)[None, :]
    expand_gam = jnp.asarray(lane == col // dk, BF16)
    expand_beta = jnp.asarray(lane == nh + col // dk, BF16)
    nat = jax.ShapeDtypeStruct((b, l, nq), F32)
    nat_spec = pl.BlockSpec((1, tm, nq), lambda bi, i: (bi, i, 0))
    const = lambda a: pl.BlockSpec(a.shape, lambda bi, i: (0, 0))
    halo_blocks = tm // SUBLANES
    return pl.pallas_call(
        functools.partial(_gdn_prep_body, tm=tm),
        grid=(b, l // tm),
        in_specs=[pl.BlockSpec((1, tm, wq), lambda bi, i: (bi, i, COL_GQKV // wq)),
                  pl.BlockSpec((1, SUBLANES, wq),
                               lambda bi, i: (bi, jnp.maximum(i * halo_blocks - 1, 0), COL_GQKV // wq)),
                  pl.BlockSpec((1, tm, LANES), lambda bi, i: (bi, i, COL_SMA // LANES)),
                  const(conv_w), const(alog), const(dtb), const(tri), const(expand_beta), const(expand_gam)],
        out_specs=[nat_spec] * 5 + [pl.BlockSpec((1, SUBLANES, tm), lambda bi, i: (bi, 0, i))],
        out_shape=[nat] * 5 + [jax.ShapeDtypeStruct((b, SUBLANES, l), F32)],
        scratch_shapes=[pltpu.VMEM((tm + SUBLANES, wq), F32)],
        compiler_params=_cparams("parallel", "arbitrary"),
        name="gdn_prep",
    )(proj3, proj3, proj3, conv_w, alog, dtb, tri, expand_beta, expand_gam)


def _split_bf16(a):
    hi = a.astype(BF16)
    lo = (a - hi.astype(F32)).astype(BF16)
    return hi, lo


def _bmm_split_rhs(a, b):
    ah = a.astype(BF16)
    bh, bl = _split_bf16(b)
    f = lambda u, v: jnp.einsum("nij,njk->nik", u, v, preferred_element_type=F32)
    return f(ah, bh) + f(ah, bl)


def _gdn_chunk_body(q_ref, k_ref, v_ref, beta_ref, gam_ref, gr_ref, z_ref, ng_ref, o_ref, s_ref,
                    *, nc):
    c = GDN_CHUNK
    nh, dk, dv = GDN_HEADS, GDN_DK, GDN_DV
    assert c == dk == dv

    @pl.when(pl.program_id(1) == 0)
    def _():
        s_ref[...] = jnp.zeros_like(s_ref)

    nb = nh * nc
    rows = lax.broadcasted_iota(jnp.int32, (nb, c, c), 1)
    cols = lax.broadcasted_iota(jnp.int32, (nb, c, c), 2)
    causal = rows >= cols
    strict = rows > cols
    def heads(ref):
        full = ref[0]
        return jnp.concatenate([full[:, h * dk:(h + 1) * dk].reshape(nc, c, dk) for h in range(nh)], axis=0)

    q, k, v = heads(q_ref), heads(k_ref), heads(v_ref)
    beta = heads(beta_ref)
    gc = heads(gam_ref)
    gr = jnp.concatenate([gr_ref[0, h:h + 1, n * c:(n + 1) * c].reshape(1, 1, c)
                          for h in range(nh) for n in range(nc)], axis=0)
    gr = jnp.broadcast_to(gr, (nb, c, c))
    decay = jnp.exp(jnp.where(causal, gc - gr, NEG_INF))
    kb = k.astype(BF16)
    qk = jnp.einsum("nid,njd->nij", jnp.concatenate([q.astype(BF16), kb], axis=1), kb,
                    preferred_element_type=F32)
    attn = qk[:, :c] * decay
    x = -jnp.where(strict, beta * qk[:, c:] * decay, 0.0)
    eg = jnp.exp(gc)
    uw = jnp.concatenate([beta * v, beta * eg * k], axis=2)
    levels = 6
    assert 2 ** levels == c
    for level in range(levels):
        if level < levels - 1:
            y = _bmm_split_rhs(x, jnp.concatenate([uw, x], axis=2))
            uw = uw + y[:, :, :2 * dv]
            x = y[:, :, 2 * dv:]
        else:
            uw = uw + _bmm_split_rhs(x, uw)
    g_last = gc[:, c - 1:c, :]
    k_dec = (k * jnp.exp(g_last - gc)).astype(BF16)
    uwb = uw.astype(BF16)
    ku = jnp.stack([lax.dot_general(k_dec[n], uwb[n], (((0,), (0,)), ((), ())), preferred_element_type=F32)
                    for n in range(nb)], axis=0)
    au = jnp.einsum("nij,njv->niv", attn.astype(BF16), uwb, preferred_element_type=F32)
    q_eff = (q * eg - au[:, :, dv:]).astype(BF16)
    split = lambda a: a.reshape((nh, nc) + a.shape[1:])
    s_in, s_map = split(ku[:, :, :dv]), split(ku[:, :, dv:].astype(BF16))
    gl = split(jnp.exp(g_last))
    s = s_ref[...]
    states = []
    for n in range(nc):
        states.append(s)
        s = gl[:, n] * s + s_in[:, n] - jnp.einsum("hkj,hjv->hkv", s_map[:, n], s.astype(BF16),
                                                  preferred_element_type=F32)
    s_ref[...] = s
    s_all = jnp.stack(states, axis=1).reshape(nb, dk, dv).astype(BF16)
    o = jnp.einsum("nik,nkv->niv", q_eff, s_all, preferred_element_type=F32) + au[:, :, :dv]
    o = split(o * lax.rsqrt(jnp.mean(o * o, axis=-1, keepdims=True) + EPS) * ng_ref[...])
    for n in range(nc):
        o_all = jnp.concatenate([o[h, n] for h in range(nh)], axis=1)
        o_ref[0, n * c:(n + 1) * c, :] = (o_all * _silu(z_ref[0, n * c:(n + 1) * c, :])).astype(o_ref.dtype)


def gdn_chunk(q, k, v, beta, gam, gr, proj3, norm_g, lt):
    b, l, wz = q.shape
    nh = GDN_HEADS
    nc = lt // GDN_CHUNK
    hm_spec = pl.BlockSpec((1, lt, wz), lambda bi, i: (bi, i, 0))
    return pl.pallas_call(
        functools.partial(_gdn_chunk_body, nc=nc),
        grid=(b, l // lt),
        in_specs=[hm_spec] * 5 + [
            pl.BlockSpec((1, SUBLANES, lt), lambda bi, i: (bi, 0, i)),
            pl.BlockSpec((1, lt, wz), lambda bi, i: (bi, i, COL_AZ // wz)),
            pl.BlockSpec((1, GDN_DV), lambda bi, i: (0, 0))],
        out_specs=pl.BlockSpec((1, lt, wz), lambda bi, i: (bi, i, 0)),
        out_shape=jax.ShapeDtypeStruct((b, l, wz), BF16),
        scratch_shapes=[pltpu.VMEM((nh, GDN_DK, GDN_DV), F32)],
        compiler_params=_cparams("parallel", "arbitrary"),
        name="gdn_chunk",
    )(q, k, v, beta, gam, gr, proj3, norm_g.reshape(1, GDN_DV))


MASK_BIAS = -1e30
LOG2E = 1.4426950408889634


def _moba_gate_body(q_ref, kall_ref, k_ref, v_ref, pq_ref, pb_ref, qa_ref, ka_ref, vb_ref, km_ref, *, nb, bpg):
    i = pl.program_id(1)
    nh, hd = MOBA_HEADS, MOBA_HD
    w = nh * hd

    assert nb & (nb - 1) == 0 and hd & (hd - 1) == 0
    nb_bits, hd_bits = nb.bit_length() - 1, hd.bit_length() - 1

    @pl.when(i == 0)
    def _():
        kmean = jnp.mean(kall_ref[0].reshape(nb, MOBA_BLOCK, w), axis=1)
        lane = lax.broadcasted_iota(jnp.int32, (nb, w), 1)
        for h in range(nh):
            km_ref[h * nb:(h + 1) * nb, :] = jnp.where(jnp.right_shift(lane, hd_bits) == h, kmean, 0.0)

    blk = lax.broadcasted_iota(jnp.int32, (nb, MOBA_BLOCK), 0)
    tile_lane = jnp.bitwise_and(lax.broadcasted_iota(jnp.int32, (MOBA_BLOCK, nh * LANES), 1), LANES - 1)
    big = jnp.int32(nb)
    for part in range(bpg):
        own = i * bpg + part
        rows = slice(part * MOBA_BLOCK, (part + 1) * MOBA_BLOCK)
        q = q_ref[0, rows, :]
        gate = lax.dot_general(km_ref[...], q, (((1,), (1,)), ((), ())),
                               precision=HIGHEST, preferred_element_type=F32)
        valid = blk < own
        masked = []
        for h in range(nh):
            g_h = gate[h * nb:(h + 1) * nb, :]
            avail = valid
            for _ in range(MOBA_TOPK):
                m = jnp.max(jnp.where(avail, g_h, NEG_INF), axis=0, keepdims=True)
                cand = avail & (g_h == m)
                first = jnp.min(jnp.where(cand, blk, big), axis=0, keepdims=True)
                avail = avail & (blk != first)
            masked.append(avail.astype(F32))
        masked = jnp.concatenate(masked, axis=0).astype(BF16)
        bias = lax.dot_general(masked, pb_ref[...], (((0,), (0,)), ((), ())), preferred_element_type=F32)
        q_tiles = _dot(q * (hd ** -0.5 * LOG2E), pq_ref[...]) + bias * MASK_BIAS
        qa_ref[0, rows, :] = q_tiles.astype(BF16)
        ka_ref[0, rows, :] = (_dot(k_ref[0, rows, :], pq_ref[...])
                              + (tile_lane == hd + own).astype(F32)).astype(BF16)
    vb_ref[0] = v_ref[0].astype(BF16)


def moba_gate(proj3):
    b, l, _ = proj3.shape
    nh, hd = MOBA_HEADS, MOBA_HD
    w = nh * hd
    nb = l // MOBA_BLOCK
    bpg = 2
    tq = bpg * MOBA_BLOCK
    assert hd + nb <= LANES
    src = np.arange(w)
    place_q = np.zeros((w, nh * LANES), np.float32)
    place_q[src, (src // hd) * LANES + src % hd] = 1.0
    srcb = np.arange(nh * nb)
    place_b = np.zeros((nh * nb, nh * LANES), np.float32)
    place_b[srcb, (srcb // nb) * LANES + hd + srcb % nb] = 1.0
    place_q, place_b = jnp.asarray(place_q, BF16), jnp.asarray(place_b, BF16)
    cq = COL_MQKV // w
    row = lambda c: pl.BlockSpec((1, tq, w), lambda bi, i: (bi, i, c))
    full = lambda a: pl.BlockSpec(a.shape, lambda bi, i: (0, 0))
    tiles = jax.ShapeDtypeStruct((b, l, nh * LANES), BF16)
    tspec = pl.BlockSpec((1, tq, nh * LANES), lambda bi, i: (bi, i, 0))
    return pl.pallas_call(
        functools.partial(_moba_gate_body, nb=nb, bpg=bpg),
        grid=(b, l // tq),
        in_specs=[row(cq), pl.BlockSpec((1, l, w), lambda bi, i: (bi, 0, cq + 1)), row(cq + 1), row(cq + 2),
                  full(place_q), full(place_b)],
        out_specs=[tspec, tspec, pl.BlockSpec((1, tq, w), lambda bi, i: (bi, i, 0))],
        out_shape=[tiles, tiles, jax.ShapeDtypeStruct((b, l, w), BF16)],
        scratch_shapes=[pltpu.VMEM((nh * nb, w), F32)],
        compiler_params=_cparams("parallel", "arbitrary"),
        name="moba_gate",
    )(proj3, proj3, proj3, proj3, place_q, place_b)


M_INIT = -2e38


def _flash_body(q_ref, k_ref, v_ref, o_ref, m_scr, acc_scr, *, nh, tq, tk):
    i = pl.program_id(1)
    row = lax.broadcasted_iota(jnp.int32, (tq, tk), 0)
    col = lax.broadcasted_iota(jnp.int32, (tq, tk), 1)
    ones = jnp.ones((tk, LANES), BF16)
    m_scr[...] = jnp.full(m_scr.shape, M_INIT, F32)
    acc_scr[...] = jnp.zeros(acc_scr.shape, F32)

    def tiles(kv_r0, diag):
        for h in range(nh):
            q = q_ref[0, :, h * LANES:(h + 1) * LANES]
            kt = k_ref[0, pl.ds(kv_r0, tk), h * LANES:(h + 1) * LANES]
            v_lo = (h // 2) * LANES
            vt = jnp.concatenate([v_ref[0, pl.ds(kv_r0, tk), v_lo:v_lo + LANES], ones], axis=1)
            s = _dot_nt(q, kt)
            if diag is not None:
                s = jnp.where(col + diag * tk <= row, s, MASK_BIAS)
            m_prev = m_scr[h]
            m_next = jnp.maximum(m_prev, jnp.max(s, axis=1, keepdims=True))
            p = jnp.exp2(s - jnp.concatenate([m_next] * (tk // LANES), axis=1))
            alpha = jnp.exp2(m_prev - m_next)
            acc_scr[h] = jnp.concatenate([alpha, alpha], axis=1) * acc_scr[h] + _dot(p, vt)
            m_scr[h] = m_next

    def past(j, carry):
        tiles(pl.multiple_of(j * tk, tk), None)
        return carry

    lax.fori_loop(0, i * (tq // tk), past, 0)
    for d in range(tq // tk):
        tiles(pl.multiple_of(i * tq + d * tk, tk), d)
    lane = lax.broadcasted_iota(jnp.int32, (tq, LANES), 1)
    low = lane < LANES // 2
    outs = [acc_scr[h][:, :LANES] / acc_scr[h][:, LANES:] for h in range(nh)]
    pairs = [jnp.where(low, outs[2 * p], outs[2 * p + 1]) for p in range(nh // 2)]
    o_ref[0] = jnp.concatenate(pairs, axis=1).astype(o_ref.dtype)


def flash_attention(q, k, v, *, nh, name):
    b, l, wq = q.shape
    wv = v.shape[2]
    tq = tk = 2 * MOBA_BLOCK
    return pl.pallas_call(
        functools.partial(_flash_body, nh=nh, tq=tq, tk=tk),
        grid=(b, l // tq),
        in_specs=[pl.BlockSpec((1, tq, wq), lambda bi, i: (bi, i, 0)),
                  pl.BlockSpec((1, l, wq), lambda bi, i: (bi, 0, 0)),
                  pl.BlockSpec((1, l, wv), lambda bi, i: (bi, 0, 0))],
        out_specs=pl.BlockSpec((1, tq, wv), lambda bi, i: (bi, i, 0)),
        out_shape=jax.ShapeDtypeStruct((b, l, wv), BF16),
        scratch_shapes=[pltpu.VMEM((nh, tq, LANES), F32), pltpu.VMEM((nh, tq, 2 * LANES), F32)],
        compiler_params=_cparams("parallel", "arbitrary"),
        name=name,
    )(q, k, v)


def _mla_proj_body(cq_ref, ckv_ref, sma_ref, smb_ref, qg_ref, kvg_ref, wq_ref, wkv_ref,
                   cq_t, sq_t, ck_t, sk_t, q_ref, k_ref, v_ref):
    nh = MLA_HEADS
    cq = cq_ref[0]
    hq = cq * lax.rsqrt(jnp.sum(cq * cq, axis=-1, keepdims=True) * (1.0 / MLA_Q_LORA) + EPS) * qg_ref[...]
    qq = _dot(hq, wq_ref[...])
    half = nh * LANES
    q_ref[0] = (qq[:, :half] * cq_t[...] + qq[:, half:] * sq_t[...]).astype(BF16)
    ckv = ckv_ref[0]
    hkv = ckv * lax.rsqrt(jnp.mean(ckv * ckv, axis=-1, keepdims=True) + EPS) * kvg_ref[...]
    kv = _dot(hkv, wkv_ref[...])
    krot = sma_ref[0] * ck_t[...] + smb_ref[0] * sk_t[...]
    k_ref[0] = (kv[:, :half] + jnp.concatenate([krot] * nh, axis=1)).astype(BF16)
    v_ref[0] = kv[:, half:].astype(BF16)


def mla_weights(q_norm_g, w_uq, kv_norm_g, w_ukv):
    nh, dn, dr, dvh = MLA_HEADS, MLA_NOPE, MLA_ROPE, MLA_V
    hr = dr // 2
    cqw = COL_CKV - COL_CQ
    wq3 = w_uq.reshape(MLA_Q_LORA, nh, dn + dr).astype(BF16)
    zeros = lambda n: jnp.zeros((MLA_Q_LORA, nh, n), BF16)
    w1 = jnp.concatenate([wq3, zeros(LANES - dn - dr)], axis=2)
    w2 = jnp.concatenate([zeros(dn), -wq3[:, :, dn + hr:], wq3[:, :, dn:dn + hr], zeros(LANES - dn - dr)], axis=2)
    wq = jnp.concatenate([w1.reshape(MLA_Q_LORA, nh * LANES), w2.reshape(MLA_Q_LORA, nh * LANES)], axis=1)
    wq = jnp.pad(wq, ((0, cqw - MLA_Q_LORA), (0, 0)))
    qg = jnp.pad(q_norm_g, (0, cqw - MLA_Q_LORA)).reshape(1, cqw)
    wkv3 = w_ukv.reshape(MLA_KV_LORA, nh, dn + dvh)
    wk = jnp.pad(wkv3[:, :, :dn], ((0, 0), (0, 0), (0, LANES - dn)))
    wkv = jnp.concatenate([wk.reshape(MLA_KV_LORA, nh * LANES),
                           wkv3[:, :, dn:].reshape(MLA_KV_LORA, nh * dvh)], axis=1).astype(BF16)
    return qg, kv_norm_g.reshape(1, MLA_KV_LORA), wq, wkv


def mla_tables(cos, sin):
    l = cos.shape[0]
    nh, dn, dr = MLA_HEADS, MLA_NOPE, MLA_ROPE
    scale = (dn + dr) ** -0.5 * LOG2E
    cc = jnp.concatenate([cos, cos], axis=1)
    ss = jnp.concatenate([sin, sin], axis=1)
    pad = jnp.zeros((l, LANES - dn - dr), F32)
    cq_tile = jnp.concatenate([jnp.ones((l, dn), F32), cc, pad], axis=1) * scale
    sq_tile = jnp.concatenate([jnp.zeros((l, dn), F32), ss, pad], axis=1) * scale
    cq_t = jnp.tile(cq_tile, (1, nh))
    sq_t = jnp.tile(sq_tile, (1, nh))
    ck_t = jnp.concatenate([jnp.zeros((l, dn), F32), cc, pad], axis=1)
    sk_t = jnp.concatenate([jnp.zeros((l, dn), F32), ss, pad], axis=1)
    return cq_t, sq_t, ck_t, sk_t


def mla_proj(proj3, weights, tables, tm):
    b, l, _ = proj3.shape
    nh, dvh = MLA_HEADS, MLA_V
    cqw = COL_CKV - COL_CQ
    qg, kvg, wq, wkv = weights
    cq_t, sq_t, ck_t, sk_t = tables
    row = lambda w_, c_: pl.BlockSpec((1, tm, w_), lambda i, bi: (bi, i, c_ // w_))
    full = lambda a: pl.BlockSpec(a.shape, lambda i, bi: (0, 0))
    tab = lambda w_: pl.BlockSpec((tm, w_), lambda i, bi: (i, 0))
    out_q = jax.ShapeDtypeStruct((b, l, nh * LANES), BF16)
    out_v = jax.ShapeDtypeStruct((b, l, nh * dvh), BF16)
    return pl.pallas_call(
        _mla_proj_body,
        grid=(l // tm, b),
        in_specs=[row(cqw, COL_CQ), row(MLA_KV_LORA, COL_CKV), row(LANES, COL_SMA), row(LANES, COL_SMB),
                  full(qg), full(kvg), full(wq), full(wkv),
                  tab(nh * LANES), tab(nh * LANES), tab(LANES), tab(LANES)],
        out_specs=[pl.BlockSpec((1, tm, nh * LANES), lambda i, bi: (bi, i, 0)),
                   pl.BlockSpec((1, tm, nh * LANES), lambda i, bi: (bi, i, 0)),
                   pl.BlockSpec((1, tm, nh * dvh), lambda i, bi: (bi, i, 0))],
        out_shape=[out_q, out_q, out_v],
        compiler_params=_cparams("parallel", "parallel"),
        name="mla_proj",
    )(proj3, proj3, proj3, proj3, qg, kvg, wq, wkv, cq_t, sq_t, ck_t, sk_t)


def _s5_state_body(u_ref, pst_ref, lam_ref, xp_ref, pu_scr, xp_scr, *, nch, nseq):
    half = S5_GROUPS * S5_STATE
    pu_scr[...] = _dot(u_ref[...], pst_ref[...])
    ar = lam_ref[0:1, :]
    ai = lam_ref[1:2, :]

    def step(kc, carry):
        nxt = []
        for sq in range(nseq):
            xr, xi = carry[sq]
            r = sq * nch + kc
            xp_scr[pl.ds(r, 1), :] = jnp.concatenate([xr, xi], axis=1)
            e = pu_scr[pl.ds(r, 1), :]
            nxt.append((ar * xr - ai * xi + e[:, :half], ar * xi + ai * xr + e[:, half:]))
        return tuple(nxt)

    zero = jnp.zeros((1, half), F32)
    lax.fori_loop(0, nch, step, tuple((zero, zero) for _ in range(nseq)))
    xp_ref[...] = xp_scr[...].astype(BF16)


def _s5_out_body(u_ref, xp_ref, kr_ref, qst_ref, y_ref):
    cn, w = S5_CHUNK, S5_WIDTH
    xp = xp_ref[...]
    for t in range(cn):
        y_ref[:, t * w:(t + 1) * w] = (_dot(u_ref[:, :(t + 1) * w], kr_ref[(cn - 1 - t) * w:, :])
                                       + _dot(xp, qst_ref[:, t * w:(t + 1) * w]))


def s5_weights(lam_re, lam_im, b_re, b_im, c_re, c_im, d, log_dt):
    cn = S5_CHUNK
    g, p, cg = S5_GROUPS, S5_STATE, S5_GROUP
    ein = functools.partial(jnp.einsum, precision=HIGHEST)
    lr = jnp.minimum(lam_re, -1e-4)
    li = lam_im
    dt = jnp.exp(log_dt)[:, None]
    mag = jnp.exp(lr * dt)
    ar, ai = mag * jnp.cos(li * dt), mag * jnp.sin(li * dt)
    den = lr * lr + li * li
    cr = ((ar - 1.0) * lr + ai * li) / den
    ci = (ai * lr - (ar - 1.0) * li) / den
    bbr = cr[..., None] * b_re - ci[..., None] * b_im
    bbi = cr[..., None] * b_im + ci[..., None] * b_re
    def powers(taus):
        tau = jnp.asarray(taus, F32)[None, :, None]
        pmag = jnp.exp((lr * dt)[:, None, :] * tau)
        pang = (li * dt)[:, None, :] * tau
        return pmag * jnp.cos(pang), pmag * jnp.sin(pang)

    def c_times(pwr, pwi):
        return (c_re[:, None] * pwr[:, :, None, :] - c_im[:, None] * pwi[:, :, None, :],
                c_re[:, None] * pwi[:, :, None, :] + c_im[:, None] * pwr[:, :, None, :])

    dsc_r, dsc_i = powers(np.arange(cn - 1, -1, -1))
    asc_r, asc_i = powers(np.arange(1, cn + 1))
    clr, cli = c_times(dsc_r, dsc_i)
    kt = ein("gtap,gpc->gtac", clr, bbr) - ein("gtap,gpc->gtac", cli, bbi)
    is_lag0 = (jnp.arange(cn) == cn - 1).astype(F32)[None, :, None, None]
    kt = kt + is_lag0 * (d[:, :, None] * jnp.eye(cg, dtype=F32)[None])[:, None]
    kc = kt.transpose(1, 0, 3, 2).reshape(cn * g * cg, cg)
    kc = jnp.pad(kc, ((0, 0), (0, LANES - cg))).astype(BF16)
    bbr_t, bbi_t = bbr.transpose(0, 2, 1)[:, None], bbi.transpose(0, 2, 1)[:, None]
    pr = dsc_r[:, :, None, :] * bbr_t - dsc_i[:, :, None, :] * bbi_t
    pi = dsc_r[:, :, None, :] * bbi_t + dsc_i[:, :, None, :] * bbr_t
    pc = jnp.stack([pr, pi], axis=3).transpose(1, 0, 2, 3, 4).reshape(cn * g * cg, 2 * p).astype(BF16)
    clr, cli = c_times(asc_r, asc_i)
    qc = jnp.stack([clr, -cli], axis=0).transpose(0, 1, 4, 2, 3).reshape(2 * g * p, cn * cg).astype(BF16)
    lam = jnp.stack([asc_r[:, cn - 1].reshape(g * p), asc_i[:, cn - 1].reshape(g * p)], axis=0)
    return kc, pc, qc, lam


def _s5_expand_body(kc_ref, pc_ref, qc_ref, ek_ref, ep_ref, eq_ref, kr_ref, pst_ref, qst_ref):
    j = pl.program_id(1)
    g_bits, p_bits, cg_bits = 4, 6, 4
    assert (1 << g_bits, 1 << p_bits, 1 << cg_bits) == (S5_GROUPS, S5_STATE, S5_GROUP)

    def spread(src_ref, e_ref, out_ref, row_shift, col_shift):
        rows, cols = out_ref.shape[1], out_ref.shape[2]
        r = lax.broadcasted_iota(jnp.int32, (rows, cols), 0) + j * rows
        c = lax.broadcasted_iota(jnp.int32, (rows, cols), 1)
        same = (jnp.bitwise_and(jnp.right_shift(r, row_shift), S5_GROUPS - 1)
                == jnp.bitwise_and(jnp.right_shift(c, col_shift), S5_GROUPS - 1))
        out_ref[0] = jnp.where(same, _dot(src_ref[0], e_ref[...]), 0.0).astype(out_ref.dtype)

    spread(kc_ref, ek_ref, kr_ref, cg_bits, cg_bits)
    spread(pc_ref, ep_ref, pst_ref, cg_bits, p_bits)
    spread(qc_ref, eq_ref, qst_ref, p_bits, cg_bits)


def s5_expand(kc, pc, qc):
    depth = kc.shape[0]
    cn, g, p, cg = S5_CHUNK, S5_GROUPS, S5_STATE, S5_GROUP
    nblk = 8
    lane = np.arange(LANES)[:, None]
    ek = jnp.asarray(lane == (np.arange(g * cg) % cg)[None, :], BF16)
    col = np.arange(2 * g * p)[None, :]
    ep = jnp.asarray(lane == (col // (g * p)) * p + col % p, BF16)
    col = np.arange(cn * g * cg)[None, :]
    eq = jnp.asarray(np.arange(cn * cg)[:, None] == (col // (g * cg)) * cg + col % cg, BF16)
    rk, rq = cn * g * cg // nblk, 2 * g * p // nblk
    blk = lambda rows, cols: pl.BlockSpec((1, rows, cols), lambda d, j: (d, j, 0))
    const = lambda a: pl.BlockSpec(a.shape, lambda d, j: (0, 0))
    return pl.pallas_call(
        _s5_expand_body,
        grid=(depth, nblk),
        in_specs=[blk(rk, LANES), blk(rk, 2 * p), blk(rq, cn * cg), const(ek), const(ep), const(eq)],
        out_specs=[blk(rk, g * cg), blk(rk, 2 * g * p), blk(rq, cn * g * cg)],
        out_shape=[jax.ShapeDtypeStruct((depth, cn * g * cg, g * cg), BF16),
                   jax.ShapeDtypeStruct((depth, cn * g * cg, 2 * g * p), BF16),
                   jax.ShapeDtypeStruct((depth, 2 * g * p, cn * g * cg), BF16)],
        compiler_params=_cparams("parallel", "parallel"),
        name="s5_expand",
    )(kc, pc, qc, ek, ep, eq)


def s5_scan(su, weights, batch):
    t, w = su.shape
    cn = S5_CHUNK
    kr, pst, qst, lam = weights
    nr = t // cn
    nch = nr // batch
    nseq = 2
    rows = nseq * nch
    u = su.reshape(nr, cn * w)
    ns = pst.shape[1]
    once = lambda a: pl.BlockSpec(a.shape, lambda i: (0, 0), pipeline_mode=pl.Buffered(1))
    rowb = lambda width: pl.BlockSpec((rows, width), lambda i: (i, 0))
    xp = pl.pallas_call(
        functools.partial(_s5_state_body, nch=nch, nseq=nseq),
        grid=(nr // rows,),
        in_specs=[rowb(cn * w), once(pst), once(lam)],
        out_specs=rowb(ns),
        out_shape=jax.ShapeDtypeStruct((nr, ns), BF16),
        scratch_shapes=[pltpu.VMEM((rows, ns), F32)] * 2,
        compiler_params=_cparams("parallel"),
        name="s5_state",
    )(u, pst, lam)
    y = pl.pallas_call(
        _s5_out_body,
        grid=(nr // rows,),
        in_specs=[rowb(cn * w), rowb(ns), once(kr), once(qst)],
        out_specs=rowb(cn * w),
        out_shape=jax.ShapeDtypeStruct((nr, cn * w), F32),
        compiler_params=_cparams("parallel"),
        name="s5_out",
    )(u, xp, kr, qst)
    return y.reshape(t, w)


def _mix_ffn_body(x_ref, oa_ref, ob_ref, oc_ref, ys_ref, xh_ref, oah_ref, obh_ref, och_ref, ysh_ref,
                  gw_ref, gb_ref, wo_ref, g_ref, wup_ref, cw_ref, wdn_ref, fg_ref, o_ref, up_scr,
                  *, tm, fc, final_norm):
    i = pl.program_id(1)
    keep = (i > 0).astype(F32)

    def norm(v, g):
        return v * lax.rsqrt(jnp.mean(v * v, axis=-1, keepdims=True) + EPS) * g

    def mixed(xr, oa, ob, oc, ys):
        y = jax.nn.gelu(ys)
        od = y * _sigmoid(_dot(y, gw_ref[...]) + gb_ref[...])
        mix = jnp.concatenate([oa, ob, oc, od.astype(BF16)], axis=1)
        return xr + _dot(mix, wo_ref[...])

    x = mixed(x_ref[0], oa_ref[0], ob_ref[0], oc_ref[0], ys_ref[0])
    x_halo = mixed(xh_ref[0], oah_ref[0], obh_ref[0], och_ref[0], ysh_ref[0])
    h_main = norm(x, g_ref[...]).astype(BF16)
    h_halo = norm(x_halo, g_ref[...]).astype(BF16)
    acc = x
    for c in range(D_FF // fc):
        acts = []
        for part in range(2):
            lo = part * D_FF + c * fc
            up_scr[0:SUBLANES, :] = _dot(h_halo, wup_ref[:, lo:lo + fc]) * keep
            up_scr[SUBLANES:, :] = _dot(h_main, wup_ref[:, lo:lo + fc])
            w = cw_ref[:, lo:lo + fc]
            y = w[FFN_CONV - 1:FFN_CONV, :] * up_scr[pl.ds(SUBLANES, tm), :]
            for tap in range(FFN_CONV - 1):
                y = y + w[tap:tap + 1, :] * up_scr[pl.ds(SUBLANES - (FFN_CONV - 1) + tap, tm), :]
            acts.append(y)
        act = _silu(acts[0]) * acts[1]
        acc = acc + _dot(act, wdn_ref[c * fc:(c + 1) * fc, :])
    o_ref[0] = norm(acc, fg_ref[...]) if final_norm else acc


def mix_ffn(x3, mixers, glu_w, glu_b, w_out, norm_g, w_up, conv_w, w_down, final_g, tm, fc):
    b, l, dm = x3.shape
    halo_blocks = tm // SUBLANES
    once = lambda a: pl.BlockSpec(a.shape, lambda bi, i: (0, 0), pipeline_mode=pl.Buffered(1))
    main = lambda a: pl.BlockSpec((1, tm, a.shape[2]), lambda bi, i: (bi, i, 0))
    halo = lambda a: pl.BlockSpec((1, SUBLANES, a.shape[2]),
                                  lambda bi, i: (bi, jnp.maximum(i * halo_blocks - 1, 0), 0))
    g = norm_g.reshape(1, dm)
    gb = glu_b.reshape(1, -1)
    final_norm = final_g is not None
    fg = final_g.reshape(1, dm) if final_norm else g
    acts = (x3,) + tuple(mixers)
    return pl.pallas_call(
        functools.partial(_mix_ffn_body, tm=tm, fc=fc, final_norm=final_norm),
        grid=(b, l // tm),
        in_specs=[main(a) for a in acts] + [halo(a) for a in acts]
        + [once(a) for a in (glu_w, gb, w_out, g, w_up, conv_w, w_down, fg)],
        out_specs=pl.BlockSpec((1, tm, dm), lambda bi, i: (bi, i, 0)),
        out_shape=jax.ShapeDtypeStruct((b, l, dm), F32),
        scratch_shapes=[pltpu.VMEM((tm + SUBLANES, fc), F32)],
        compiler_params=_cparams("parallel", "arbitrary"),
        name="mix_ffn",
    )(*acts, *acts, glu_w, gb, w_out, g, w_up, conv_w, w_down, fg)


def _permute_w_in(w_in):
    dm = w_in.shape[0]
    gw, mw = GDN_HEADS * GDN_DK, MOBA_HEADS * MOBA_HD
    o_aq, o_az, o_aa, o_ab = 0, 3 * gw, 4 * gw, 4 * gw + GDN_HEADS
    o_mq = o_ab + GDN_HEADS
    o_cq = o_mq + 3 * mw
    o_ckv = o_cq + MLA_Q_LORA
    o_ckr = o_ckv + MLA_KV_LORA
    o_su = o_ckr + MLA_ROPE
    hr = MLA_ROPE // 2
    z = lambda n: jnp.zeros((dm, n), w_in.dtype)
    ckr = w_in[:, o_ckr:o_ckr + MLA_ROPE]
    ckr_swap = jnp.concatenate([-ckr[:, hr:], ckr[:, :hr]], axis=1)
    cols = [w_in[:, o_aq:o_aq + 3 * gw],
            w_in[:, o_mq:o_mq + 3 * mw],
            w_in[:, o_az:o_az + gw],
            w_in[:, o_su:o_su + S5_WIDTH],
            w_in[:, o_cq:o_cq + MLA_Q_LORA], z(COL_CKV - COL_CQ - MLA_Q_LORA),
            w_in[:, o_ckv:o_ckv + MLA_KV_LORA],
            w_in[:, o_aa:o_aa + 2 * GDN_HEADS], z(MLA_NOPE - 2 * GDN_HEADS), ckr, z(LANES - MLA_NOPE - MLA_ROPE),
            z(MLA_NOPE), ckr_swap, z(LANES - MLA_NOPE - MLA_ROPE)]
    return jnp.concatenate(cols, axis=1).astype(BF16)


def _layer(x3, tables, w, final_g):
    b, l, dm = x3.shape
    t = b * l
    x2 = x3.reshape(t, dm)
    proj, su = in_proj(x2, w["norm1_g"], w["w_in"], ROW_TILE)
    proj3 = proj.reshape(b, l, N_PROJ)

    q, k, v, beta, gam, gr = gdn_prep(proj3, w["gdn_conv_w"], w["gdn_alog"], w["gdn_dtb"], ROW_TILE)
    o_a = gdn_chunk(q, k, v, beta, gam, gr, proj3, w["gdn_norm_g"], ROW_TILE)

    mq, mk, mv = moba_gate(proj3)
    o_b = flash_attention(mq, mk, mv, nh=MOBA_HEADS, name="flash_moba")

    cq, ck, cv = mla_proj(proj3, w["mla"], tables, ROW_TILE)
    o_c = flash_attention(cq, ck, cv, nh=MLA_HEADS, name="flash_mla")

    ys = s5_scan(su, w["s5"], b).reshape(b, l, S5_WIDTH)

    return mix_ffn(x3, (o_a, o_b, o_c, ys), w["s5_glu_w"], w["s5_glu_b"], w["w_out"], w["norm2_g"],
                   w["ffn_w_up"], w["ffn_conv_w"], w["ffn_w_down"], final_g, FFN_ROW_TILE, D_FF // 2)


def _prepare_weights(p):
    nh = GDN_HEADS
    return {
        "norm1_g": p["norm1_g"], "w_in": _permute_w_in(p["w_in"]),
        "gdn_conv_w": p["gdn_conv_w"], "gdn_norm_g": p["gdn_norm_g"],
        "gdn_alog": jnp.pad(p["gdn_a_log"], (0, LANES - nh)).reshape(1, LANES),
        "gdn_dtb": jnp.pad(p["gdn_dt_bias"], (0, LANES - nh)).reshape(1, LANES),
        "mla": mla_weights(p["mla_q_norm_g"], p["mla_w_uq"], p["mla_kv_norm_g"], p["mla_w_ukv"]),
        "s5": s5_weights(p["s5_lam_re"], p["s5_lam_im"], p["s5_b_re"], p["s5_b_im"], p["s5_c_re"], p["s5_c_im"],
                         p["s5_d"], p["s5_log_dt"]),
        "s5_glu_w": p["s5_glu_w"].astype(BF16), "s5_glu_b": p["s5_glu_b"], "w_out": p["w_out"].astype(BF16),
        "norm2_g": p["norm2_g"], "ffn_w_up": p["ffn_w_up"].astype(BF16), "ffn_conv_w": p["ffn_conv_w"],
        "ffn_w_down": p["ffn_w_down"].astype(BF16),
    }


_LAYER_KEYS = ("norm1_g", "w_in", "gdn_conv_w", "gdn_a_log", "gdn_dt_bias", "gdn_norm_g",
               "mla_q_norm_g", "mla_w_uq", "mla_kv_norm_g", "mla_w_ukv",
               "s5_lam_re", "s5_lam_im", "s5_b_re", "s5_b_im", "s5_c_re", "s5_c_im", "s5_d", "s5_log_dt",
               "s5_glu_w", "s5_glu_b", "w_out", "norm2_g", "ffn_w_up", "ffn_conv_w", "ffn_w_down")


def kernel(x, norm1_g, w_in, gdn_conv_w, gdn_a_log, gdn_dt_bias, gdn_norm_g, mla_q_norm_g, mla_w_uq,
           mla_kv_norm_g, mla_w_ukv, s5_lam_re, s5_lam_im, s5_b_re, s5_b_im, s5_c_re, s5_c_im, s5_d,
           s5_log_dt, s5_glu_w, s5_glu_b, w_out, norm2_g, ffn_w_up, ffn_conv_w, ffn_w_down, final_norm_g):
    stacked = (norm1_g, w_in, gdn_conv_w, gdn_a_log, gdn_dt_bias, gdn_norm_g, mla_q_norm_g, mla_w_uq,
               mla_kv_norm_g, mla_w_ukv, s5_lam_re, s5_lam_im, s5_b_re, s5_b_im, s5_c_re, s5_c_im, s5_d,
               s5_log_dt, s5_glu_w, s5_glu_b, w_out, norm2_g, ffn_w_up, ffn_conv_w, ffn_w_down)
    l = x.shape[1]
    inv_freq = ROPE_THETA ** (-jnp.arange(0, MLA_ROPE, 2, dtype=F32) / MLA_ROPE)
    ang = jnp.arange(l, dtype=F32)[:, None] * inv_freq[None, :]
    tables = mla_tables(jnp.cos(ang), jnp.sin(ang))
    weights = jax.vmap(_prepare_weights)(dict(zip(_LAYER_KEYS, stacked)))
    kc, pc, qc, lam = weights["s5"]
    weights["s5"] = (*s5_expand(kc, pc, qc), lam)
    depth = norm1_g.shape[0]
    for i in range(depth):
        w = jax.tree_util.tree_map(lambda a: a[i], weights)
        x = _layer(x, tables, w, final_norm_g if i == depth - 1 else None)
    return x
```

```python
import functools
import math

import jax
import jax.numpy as jnp
import numpy as np
from jax import lax
from jax.experimental import pallas as pl
from jax.experimental.pallas import tpu as pltpu

F32 = jnp.float32
BF16 = jnp.bfloat16
NEG_INF = float("-inf")
EPS = 1e-6
HIGHEST = lax.Precision.HIGHEST

D_MODEL = 1024
GDN_HEADS, GDN_DK, GDN_DV, GDN_CONV, GDN_CHUNK = 4, 64, 64, 4, 64
MOBA_HEADS, MOBA_HD, MOBA_BLOCK, MOBA_TOPK = 4, 64, 256, 3
MLA_HEADS, MLA_Q_LORA, MLA_KV_LORA, MLA_NOPE, MLA_ROPE, MLA_V = 4, 192, 128, 64, 32, 64
ROPE_THETA = 10000.0
S5_GROUP, S5_GROUPS, S5_STATE = 16, 16, 64
S5_WIDTH = S5_GROUP * S5_GROUPS
D_FF, FFN_CONV = 2816, 3

LANES = 128
SUBLANES = 8
VMEM_LIMIT = 60 * 1024 * 1024

COL_GQKV, COL_MQKV, COL_AZ, COL_SU, COL_CQ, COL_CKV, COL_SMA, COL_SMB = (
    0, 768, 1536, 1792, 2048, 2304, 2432, 2560)
N_PROJ = 2688
S5_CHUNK = 16
ROW_TILE = 512
FFN_ROW_TILE = 1024


def _cparams(*sems):
    return pltpu.CompilerParams(dimension_semantics=sems, vmem_limit_bytes=VMEM_LIMIT)


def _dot(a, b):
    return jnp.dot(a.astype(BF16), b.astype(BF16), preferred_element_type=F32)


def _dot_nt(a, b):
    return lax.dot_general(a.astype(BF16), b.astype(BF16), (((1,), (1,)), ((), ())),
                           preferred_element_type=F32)


def _split3_bf16(x):
    hi = x.astype(BF16)
    r1 = x - hi.astype(F32)
    mid = r1.astype(BF16)
    return hi, mid, (r1 - mid.astype(F32)).astype(BF16)


def _sigmoid(x):
    return jax.nn.sigmoid(x)


def _silu(x):
    return x * jax.nn.sigmoid(x)


def _in_proj_body(x_ref, g_ref, w_ref, o_ref, su_ref, *, inv_dim):
    x = x_ref[...]
    ms = jnp.sum(x * x, axis=-1, keepdims=True) * inv_dim
    h = x * lax.rsqrt(ms + EPS) * g_ref[...]
    proj = _dot(h, w_ref[...])
    o_ref[...] = proj
    su_ref[...] = proj[:, COL_SU:COL_SU + S5_WIDTH].astype(BF16)


def in_proj(x, g, w, tm):
    t, k = x.shape
    n = w.shape[1]
    return pl.pallas_call(
        functools.partial(_in_proj_body, inv_dim=1.0 / k),
        grid=(t // tm,),
        in_specs=[pl.BlockSpec((tm, k), lambda i: (i, 0)),
                  pl.BlockSpec((1, k), lambda i: (0, 0)),
                  pl.BlockSpec((k, n), lambda i: (0, 0))],
        out_specs=[pl.BlockSpec((tm, n), lambda i: (i, 0)), pl.BlockSpec((tm, S5_WIDTH), lambda i: (i, 0))],
        out_shape=[jax.ShapeDtypeStruct((t, n), F32), jax.ShapeDtypeStruct((t, S5_WIDTH), BF16)],
        compiler_params=_cparams("parallel"),
        name="in_proj",
    )(x, g.reshape(1, k), w)


def _gdn_prep_body(qkv_ref, halo_ref, sm_ref, cw_ref, alog_ref, dtb_ref, tri_ref, eb_ref, eg_ref,
                   q_ref, k_ref, v_ref, beta_ref, gam_ref, gr_ref, xs_ref, *, tm):
    i = pl.program_id(1)
    keep = (i > 0).astype(F32)
    xs_ref[0:SUBLANES, :] = halo_ref[0] * keep
    xs_ref[SUBLANES:, :] = qkv_ref[0]
    w = cw_ref[...]
    nq = GDN_HEADS * GDN_DK
    dk = GDN_DK
    y = w[3:4, :] * xs_ref[pl.ds(SUBLANES, tm), :]
    for tap in range(GDN_CONV - 1):
        y = y + w[tap:tap + 1, :] * xs_ref[pl.ds(SUBLANES - (GDN_CONV - 1) + tap, tm), :]
    y = _silu(y)
    for h in range(GDN_HEADS):
        lanes = slice(h * dk, (h + 1) * dk)
        qh = y[:, h * dk:(h + 1) * dk]
        kh = y[:, nq + h * dk:nq + (h + 1) * dk]
        q_ref[0, :, lanes] = qh * lax.rsqrt(jnp.sum(qh * qh, axis=-1, keepdims=True) + EPS) * (dk ** -0.5)
        k_ref[0, :, lanes] = kh * lax.rsqrt(jnp.sum(kh * kh, axis=-1, keepdims=True) + EPS)
    v_ref[0] = y[:, 2 * nq:]
    sm = sm_ref[0]
    g = -jnp.exp(alog_ref[...]) * jax.nn.softplus(sm + dtb_ref[...])
    beta = _sigmoid(sm)
    tri = tri_ref[...]
    gam = sum(jnp.dot(tri, part, preferred_element_type=F32) for part in _split3_bf16(g))
    pick = (lax.broadcasted_iota(jnp.int32, (SUBLANES, LANES), 0)
            == lax.broadcasted_iota(jnp.int32, (SUBLANES, LANES), 1)).astype(BF16)
    gr_ref[0] = sum(lax.dot_general(pick, part, (((1,), (1,)), ((), ())), preferred_element_type=F32)
                    for part in _split3_bf16(gam))
    beta_ref[0] = sum(jnp.dot(part, eb_ref[...], preferred_element_type=F32) for part in _split3_bf16(beta))
    gam_ref[0] = sum(jnp.dot(part, eg_ref[...], preferred_element_type=F32) for part in _split3_bf16(gam))


def gdn_prep(proj3, conv_w, alog, dtb, tm):
    b, l, _ = proj3.shape
    nh, dk = GDN_HEADS, GDN_DK
    nq = nh * dk
    wq = 3 * nq
    r = np.arange(tm)
    tri = jnp.asarray((r[:, None] // GDN_CHUNK == r[None, :] // GDN_CHUNK) & (r[None, :] <= r[:, None]), BF16)
    lane, col = np.arange(LANES)[:, None], np.arange(nq)[None, :]
    expand_gam = jnp.asarray(lane == col // dk, BF16)
    expand_beta = jnp.asarray(lane == nh + col // dk, BF16)
    nat = jax.ShapeDtypeStruct((b, l, nq), F32)
    nat_spec = pl.BlockSpec((1, tm, nq), lambda bi, i: (bi, i, 0))
    const = lambda a: pl.BlockSpec(a.shape, lambda bi, i: (0, 0))
    halo_blocks = tm // SUBLANES
    return pl.pallas_call(
        functools.partial(_gdn_prep_body, tm=tm),
        grid=(b, l // tm),
        in_specs=[pl.BlockSpec((1, tm, wq), lambda bi, i: (bi, i, COL_GQKV // wq)),
                  pl.BlockSpec((1, SUBLANES, wq),
                               lambda bi, i: (bi, jnp.maximum(i * halo_blocks - 1, 0), COL_GQKV // wq)),
                  pl.BlockSpec((1, tm, LANES), lambda bi, i: (bi, i, COL_SMA // LANES)),
                  const(conv_w), const(alog), const(dtb), const(tri), const(expand_beta), const(expand_gam)],
        out_specs=[nat_spec] * 5 + [pl.BlockSpec((1, SUBLANES, tm), lambda bi, i: (bi, 0, i))],
        out_shape=[nat] * 5 + [jax.ShapeDtypeStruct((b, SUBLANES, l), F32)],
        scratch_shapes=[pltpu.VMEM((tm + SUBLANES, wq), F32)],
        compiler_params=_cparams("parallel", "arbitrary"),
        name="gdn_prep",
    )(proj3, proj3, proj3, conv_w, alog, dtb, tri, expand_beta, expand_gam)


def _split_bf16(a):
    hi = a.astype(BF16)
    lo = (a - hi.astype(F32)).astype(BF16)
    return hi, lo


def _bmm_split_rhs(a, b):
    ah = a.astype(BF16)
    bh, bl = _split_bf16(b)
    f = lambda u, v: jnp.einsum("nij,njk->nik", u, v, preferred_element_type=F32)
    return f(ah, bh) + f(ah, bl)


def _gdn_chunk_body(q_ref, k_ref, v_ref, beta_ref, gam_ref, gr_ref, z_ref, ng_ref, o_ref, s_ref,
                    *, nc):
    c = GDN_CHUNK
    nh, dk, dv = GDN_HEADS, GDN_DK, GDN_DV
    assert c == dk == dv

    @pl.when(pl.program_id(1) == 0)
    def _():
        s_ref[...] = jnp.zeros_like(s_ref)

    nb = nh * nc
    rows = lax.broadcasted_iota(jnp.int32, (nb, c, c), 1)
    cols = lax.broadcasted_iota(jnp.int32, (nb, c, c), 2)
    causal = rows >= cols
    strict = rows > cols
    def heads(ref):
        full = ref[0]
        return jnp.concatenate([full[:, h * dk:(h + 1) * dk].reshape(nc, c, dk) for h in range(nh)], axis=0)

    q, k, v = heads(q_ref), heads(k_ref), heads(v_ref)
    beta = heads(beta_ref)
    gc = heads(gam_ref)
    gr = jnp.concatenate([gr_ref[0, h:h + 1, n * c:(n + 1) * c].reshape(1, 1, c)
                          for h in range(nh) for n in range(nc)], axis=0)
    gr = jnp.broadcast_to(gr, (nb, c, c))
    decay = jnp.exp(jnp.where(causal, gc - gr, NEG_INF))
    kb = k.astype(BF16)
    qk = jnp.einsum("nid,njd->nij", jnp.concatenate([q.astype(BF16), kb], axis=1), kb,
                    preferred_element_type=F32)
    attn = qk[:, :c] * decay
    x = -jnp.where(strict, beta * qk[:, c:] * decay, 0.0)
    eg = jnp.exp(gc)
    uw = jnp.concatenate([beta * v, beta * eg * k], axis=2)
    levels = 6
    assert 2 ** levels == c
    for level in range(levels):
        if level < levels - 1:
            y = _bmm_split_rhs(x, jnp.concatenate([uw, x], axis=2))
            uw = uw + y[:, :, :2 * dv]
            x = y[:, :, 2 * dv:]
        else:
            uw = uw + _bmm_split_rhs(x, uw)
    g_last = gc[:, c - 1:c, :]
    k_dec = (k * jnp.exp(g_last - gc)).astype(BF16)
    uwb = uw.astype(BF16)
    ku = jnp.stack([lax.dot_general(k_dec[n], uwb[n], (((0,), (0,)), ((), ())), preferred_element_type=F32)
                    for n in range(nb)], axis=0)
    au = jnp.einsum("nij,njv->niv", attn.astype(BF16), uwb, preferred_element_type=F32)
    q_eff = (q * eg - au[:, :, dv:]).astype(BF16)
    split = lambda a: a.reshape((nh, nc) + a.shape[1:])
    s_in, s_map = split(ku[:, :, :dv]), split(ku[:, :, dv:].astype(BF16))
    gl = split(jnp.exp(g_last))
    s = s_ref[...]
    states = []
    for n in range(nc):
        states.append(s)
        s = gl[:, n] * s + s_in[:, n] - jnp.einsum("hkj,hjv->hkv", s_map[:, n], s.astype(BF16),
                                                  preferred_element_type=F32)
    s_ref[...] = s
    s_all = jnp.stack(states, axis=1).reshape(nb, dk, dv).astype(BF16)
    o = jnp.einsum("nik,nkv->niv", q_eff, s_all, preferred_element_type=F32) + au[:, :, :dv]
    o = split(o * lax.rsqrt(jnp.mean(o * o, axis=-1, keepdims=True) + EPS) * ng_ref[...])
    for n in range(nc):
        o_all = jnp.concatenate([o[h, n] for h in range(nh)], axis=1)
        o_ref[0, n * c:(n + 1) * c, :] = (o_all * _silu(z_ref[0, n * c:(n + 1) * c, :])).astype(o_ref.dtype)


def gdn_chunk(q, k, v, beta, gam, gr, proj3, norm_g, lt):
    b, l, wz = q.shape
    nh = GDN_HEADS
    nc = lt // GDN_CHUNK
    hm_spec = pl.BlockSpec((1, lt, wz), lambda bi, i: (bi, i, 0))
    return pl.pallas_call(
        functools.partial(_gdn_chunk_body, nc=nc),
        grid=(b, l // lt),
        in_specs=[hm_spec] * 5 + [
            pl.BlockSpec((1, SUBLANES, lt), lambda bi, i: (bi, 0, i)),
            pl.BlockSpec((1, lt, wz), lambda bi, i: (bi, i, COL_AZ // wz)),
            pl.BlockSpec((1, GDN_DV), lambda bi, i: (0, 0))],
        out_specs=pl.BlockSpec((1, lt, wz), lambda bi, i: (bi, i, 0)),
        out_shape=jax.ShapeDtypeStruct((b, l, wz), BF16),
        scratch_shapes=[pltpu.VMEM((nh, GDN_DK, GDN_DV), F32)],
        compiler_params=_cparams("parallel", "arbitrary"),
        name="gdn_chunk",
    )(q, k, v, beta, gam, gr, proj3, norm_g.reshape(1, GDN_DV))


MASK_BIAS = -1e30
LOG2E = 1.4426950408889634


def _moba_gate_body(q_ref, kall_ref, k_ref, v_ref, pq_ref, pb_ref, qa_ref, ka_ref, vb_ref, km_ref, *, nb, bpg):
    i = pl.program_id(1)
    nh, hd = MOBA_HEADS, MOBA_HD
    w = nh * hd

    assert nb & (nb - 1) == 0 and hd & (hd - 1) == 0
    nb_bits, hd_bits = nb.bit_length() - 1, hd.bit_length() - 1

    @pl.when(i == 0)
    def _():
        kmean = jnp.mean(kall_ref[0].reshape(nb, MOBA_BLOCK, w), axis=1)
        lane = lax.broadcasted_iota(jnp.int32, (nb, w), 1)
        for h in range(nh):
            km_ref[h * nb:(h + 1) * nb, :] = jnp.where(jnp.right_shift(lane, hd_bits) == h, kmean, 0.0)

    blk = lax.broadcasted_iota(jnp.int32, (nb, MOBA_BLOCK), 0)
    tile_lane = jnp.bitwise_and(lax.broadcasted_iota(jnp.int32, (MOBA_BLOCK, nh * LANES), 1), LANES - 1)
    big = jnp.int32(nb)
    for part in range(bpg):
        own = i * bpg + part
        rows = slice(part * MOBA_BLOCK, (part + 1) * MOBA_BLOCK)
        q = q_ref[0, rows, :]
        gate = lax.dot_general(km_ref[...], q, (((1,), (1,)), ((), ())),
                               precision=HIGHEST, preferred_element_type=F32)
        valid = blk < own
        masked = []
        for h in range(nh):
            g_h = gate[h * nb:(h + 1) * nb, :]
            avail = valid
            for _ in range(MOBA_TOPK):
                m = jnp.max(jnp.where(avail, g_h, NEG_INF), axis=0, keepdims=True)
                cand = avail & (g_h == m)
                first = jnp.min(jnp.where(cand, blk, big), axis=0, keepdims=True)
                avail = avail & (blk != first)
            masked.append(avail.astype(F32))
        masked = jnp.concatenate(masked, axis=0).astype(BF16)
        bias = lax.dot_general(masked, pb_ref[...], (((0,), (0,)), ((), ())), preferred_element_type=F32)
        q_tiles = _dot(q * (hd ** -0.5 * LOG2E), pq_ref[...]) + bias * MASK_BIAS
        qa_ref[0, rows, :] = q_tiles.astype(BF16)
        ka_ref[0, rows, :] = (_dot(k_ref[0, rows, :], pq_ref[...])
                              + (tile_lane == hd + own).astype(F32)).astype(BF16)
    vb_ref[0] = v_ref[0].astype(BF16)


def moba_gate(proj3):
    b, l, _ = proj3.shape
    nh, hd = MOBA_HEADS, MOBA_HD
    w = nh * hd
    nb = l // MOBA_BLOCK
    bpg = 2
    tq = bpg * MOBA_BLOCK
    assert hd + nb <= LANES
    src = np.arange(w)
    place_q = np.zeros((w, nh * LANES), np.float32)
    place_q[src, (src // hd) * LANES + src % hd] = 1.0
    srcb = np.arange(nh * nb)
    place_b = np.zeros((nh * nb, nh * LANES), np.float32)
    place_b[srcb, (srcb // nb) * LANES + hd + srcb % nb] = 1.0
    place_q, place_b = jnp.asarray(place_q, BF16), jnp.asarray(place_b, BF16)
    cq = COL_MQKV // w
    row = lambda c: pl.BlockSpec((1, tq, w), lambda bi, i: (bi, i, c))
    full = lambda a: pl.BlockSpec(a.shape, lambda bi, i: (0, 0))
    tiles = jax.ShapeDtypeStruct((b, l, nh * LANES), BF16)
    tspec = pl.BlockSpec((1, tq, nh * LANES), lambda bi, i: (bi, i, 0))
    return pl.pallas_call(
        functools.partial(_moba_gate_body, nb=nb, bpg=bpg),
        grid=(b, l // tq),
        in_specs=[row(cq), pl.BlockSpec((1, l, w), lambda bi, i: (bi, 0, cq + 1)), row(cq + 1), row(cq + 2),
                  full(place_q), full(place_b)],
        out_specs=[tspec, tspec, pl.BlockSpec((1, tq, w), lambda bi, i: (bi, i, 0))],
        out_shape=[tiles, tiles, jax.ShapeDtypeStruct((b, l, w), BF16)],
        scratch_shapes=[pltpu.VMEM((nh * nb, w), F32)],
        compiler_params=_cparams("parallel", "arbitrary"),
        name="moba_gate",
    )(proj3, proj3, proj3, proj3, place_q, place_b)


M_INIT = -2e38


def _flash_body(q_ref, k_ref, v_ref, o_ref, m_scr, acc_scr, *, nh, tq, tk):
    i = pl.program_id(1)
    m_scr[...] = jnp.full(m_scr.shape, M_INIT, F32)
    acc_scr[...] = jnp.zeros(acc_scr.shape, F32)

    def tiles(kv_r0, nk, row0, causal):
        nq = tq - row0
        ones = jnp.ones((nk, LANES), BF16)
        for h in range(nh):
            q = q_ref[0, row0:, h * LANES:(h + 1) * LANES]
            kt = k_ref[0, pl.ds(kv_r0, nk), h * LANES:(h + 1) * LANES]
            v_lo = (h // 2) * LANES
            vt = jnp.concatenate([v_ref[0, pl.ds(kv_r0, nk), v_lo:v_lo + LANES], ones], axis=1)
            s = _dot_nt(q, kt)
            if causal:
                row = lax.broadcasted_iota(jnp.int32, (nq, nk), 0)
                col = lax.broadcasted_iota(jnp.int32, (nq, nk), 1)
                s = jnp.where(col <= row, s, MASK_BIAS)
            m_prev = m_scr[h, row0:, :]
            m_next = jnp.maximum(m_prev, jnp.max(s, axis=1, keepdims=True))
            p = jnp.exp2(s - jnp.concatenate([m_next] * (nk // LANES), axis=1))
            alpha = jnp.exp2(m_prev - m_next)
            acc_scr[h, row0:, :] = jnp.concatenate([alpha, alpha], axis=1) * acc_scr[h, row0:, :] + _dot(p, vt)
            m_scr[h, row0:, :] = m_next

    def past(j, carry):
        tiles(pl.multiple_of(j * tk, tk), tk, 0, False)
        return carry

    lax.fori_loop(0, i * (tq // tk), past, 0)
    sub = tq // 2
    for d in range(tq // sub):
        tiles(pl.multiple_of(i * tq + d * sub, sub), sub, d * sub, True)
    lane = lax.broadcasted_iota(jnp.int32, (tq, LANES), 1)
    low = lane < LANES // 2
    outs = [acc_scr[h][:, :LANES] / acc_scr[h][:, LANES:] for h in range(nh)]
    pairs = [jnp.where(low, outs[2 * p], outs[2 * p + 1]) for p in range(nh // 2)]
    o_ref[0] = jnp.concatenate(pairs, axis=1).astype(o_ref.dtype)


def flash_attention(q, k, v, *, nh, name):
    b, l, wq = q.shape
    wv = v.shape[2]
    tq = tk = 2 * MOBA_BLOCK
    return pl.pallas_call(
        functools.partial(_flash_body, nh=nh, tq=tq, tk=tk),
        grid=(b, l // tq),
        in_specs=[pl.BlockSpec((1, tq, wq), lambda bi, i: (bi, i, 0)),
                  pl.BlockSpec((1, l, wq), lambda bi, i: (bi, 0, 0)),
                  pl.BlockSpec((1, l, wv), lambda bi, i: (bi, 0, 0))],
        out_specs=pl.BlockSpec((1, tq, wv), lambda bi, i: (bi, i, 0)),
        out_shape=jax.ShapeDtypeStruct((b, l, wv), BF16),
        scratch_shapes=[pltpu.VMEM((nh, tq, LANES), F32), pltpu.VMEM((nh, tq, 2 * LANES), F32)],
        compiler_params=_cparams("parallel", "arbitrary"),
        name=name,
    )(q, k, v)


def _mla_proj_body(cq_ref, ckv_ref, sma_ref, smb_ref, qg_ref, kvg_ref, wq_ref, wkv_ref,
                   cq_t, sq_t, ck_t, sk_t, q_ref, k_ref, v_ref):
    nh = MLA_HEADS
    cq = cq_ref[0]
    hq = cq * lax.rsqrt(jnp.sum(cq * cq, axis=-1, keepdims=True) * (1.0 / MLA_Q_LORA) + EPS) * qg_ref[...]
    qq = _dot(hq, wq_ref[...])
    half = nh * LANES
    q_ref[0] = (qq[:, :half] * cq_t[...] + qq[:, half:] * sq_t[...]).astype(BF16)
    ckv = ckv_ref[0]
    hkv = ckv * lax.rsqrt(jnp.mean(ckv * ckv, axis=-1, keepdims=True) + EPS) * kvg_ref[...]
    kv = _dot(hkv, wkv_ref[...])
    krot = sma_ref[0] * ck_t[...] + smb_ref[0] * sk_t[...]
    k_ref[0] = (kv[:, :half] + jnp.concatenate([krot] * nh, axis=1)).astype(BF16)
    v_ref[0] = kv[:, half:].astype(BF16)


def mla_weights(q_norm_g, w_uq, kv_norm_g, w_ukv):
    nh, dn, dr, dvh = MLA_HEADS, MLA_NOPE, MLA_ROPE, MLA_V
    hr = dr // 2
    cqw = COL_CKV - COL_CQ
    wq3 = w_uq.reshape(MLA_Q_LORA, nh, dn + dr).astype(BF16)
    zeros = lambda n: jnp.zeros((MLA_Q_LORA, nh, n), BF16)
    w1 = jnp.concatenate([wq3, zeros(LANES - dn - dr)], axis=2)
    w2 = jnp.concatenate([zeros(dn), -wq3[:, :, dn + hr:], wq3[:, :, dn:dn + hr], zeros(LANES - dn - dr)], axis=2)
    wq = jnp.concatenate([w1.reshape(MLA_Q_LORA, nh * LANES), w2.reshape(MLA_Q_LORA, nh * LANES)], axis=1)
    wq = jnp.pad(wq, ((0, cqw - MLA_Q_LORA), (0, 0)))
    qg = jnp.pad(q_norm_g, (0, cqw - MLA_Q_LORA)).reshape(1, cqw)
    wkv3 = w_ukv.reshape(MLA_KV_LORA, nh, dn + dvh)
    wk = jnp.pad(wkv3[:, :, :dn], ((0, 0), (0, 0), (0, LANES - dn)))
    wkv = jnp.concatenate([wk.reshape(MLA_KV_LORA, nh * LANES),
                           wkv3[:, :, dn:].reshape(MLA_KV_LORA, nh * dvh)], axis=1).astype(BF16)
    return qg, kv_norm_g.reshape(1, MLA_KV_LORA), wq, wkv


def mla_tables(cos, sin):
    l = cos.shape[0]
    nh, dn, dr = MLA_HEADS, MLA_NOPE, MLA_ROPE
    scale = (dn + dr) ** -0.5 * LOG2E
    cc = jnp.concatenate([cos, cos], axis=1)
    ss = jnp.concatenate([sin, sin], axis=1)
    pad = jnp.zeros((l, LANES - dn - dr), F32)
    cq_tile = jnp.concatenate([jnp.ones((l, dn), F32), cc, pad], axis=1) * scale
    sq_tile = jnp.concatenate([jnp.zeros((l, dn), F32), ss, pad], axis=1) * scale
    cq_t = jnp.tile(cq_tile, (1, nh))
    sq_t = jnp.tile(sq_tile, (1, nh))
    ck_t = jnp.concatenate([jnp.zeros((l, dn), F32), cc, pad], axis=1)
    sk_t = jnp.concatenate([jnp.zeros((l, dn), F32), ss, pad], axis=1)
    return cq_t, sq_t, ck_t, sk_t


def mla_proj(proj3, weights, tables, tm):
    b, l, _ = proj3.shape
    nh, dvh = MLA_HEADS, MLA_V
    cqw = COL_CKV - COL_CQ
    qg, kvg, wq, wkv = weights
    cq_t, sq_t, ck_t, sk_t = tables
    row = lambda w_, c_: pl.BlockSpec((1, tm, w_), lambda i, bi: (bi, i, c_ // w_))
    full = lambda a: pl.BlockSpec(a.shape, lambda i, bi: (0, 0))
    tab = lambda w_: pl.BlockSpec((tm, w_), lambda i, bi: (i, 0))
    out_q = jax.ShapeDtypeStruct((b, l, nh * LANES), BF16)
    out_v = jax.ShapeDtypeStruct((b, l, nh * dvh), BF16)
    return pl.pallas_call(
        _mla_proj_body,
        grid=(l // tm, b),
        in_specs=[row(cqw, COL_CQ), row(MLA_KV_LORA, COL_CKV), row(LANES, COL_SMA), row(LANES, COL_SMB),
                  full(qg), full(kvg), full(wq), full(wkv),
                  tab(nh * LANES), tab(nh * LANES), tab(LANES), tab(LANES)],
        out_specs=[pl.BlockSpec((1, tm, nh * LANES), lambda i, bi: (bi, i, 0)),
                   pl.BlockSpec((1, tm, nh * LANES), lambda i, bi: (bi, i, 0)),
                   pl.BlockSpec((1, tm, nh * dvh), lambda i, bi: (bi, i, 0))],
        out_shape=[out_q, out_q, out_v],
        compiler_params=_cparams("parallel", "parallel"),
        name="mla_proj",
    )(proj3, proj3, proj3, proj3, qg, kvg, wq, wkv, cq_t, sq_t, ck_t, sk_t)


def _s5_state_body(u_ref, pst_ref, lam_ref, xp_ref, pu_scr, xp_scr, *, nch, nseq):
    half = S5_GROUPS * S5_STATE
    pu_scr[...] = _dot(u_ref[...], pst_ref[...])
    ar = lam_ref[0:1, :]
    ai = lam_ref[1:2, :]

    def step(kc, carry):
        nxt = []
        for sq in range(nseq):
            xr, xi = carry[sq]
            r = sq * nch + kc
            xp_scr[pl.ds(r, 1), :] = jnp.concatenate([xr, xi], axis=1)
            e = pu_scr[pl.ds(r, 1), :]
            nxt.append((ar * xr - ai * xi + e[:, :half], ar * xi + ai * xr + e[:, half:]))
        return tuple(nxt)

    zero = jnp.zeros((1, half), F32)
    lax.fori_loop(0, nch, step, tuple((zero, zero) for _ in range(nseq)))
    xp_ref[...] = xp_scr[...].astype(BF16)


def _s5_out_body(u_ref, xp_ref, kr_ref, qst_ref, y_ref):
    cn, w = S5_CHUNK, S5_WIDTH
    xp = xp_ref[...]
    for t in range(cn):
        y_ref[:, t * w:(t + 1) * w] = (_dot(u_ref[:, :(t + 1) * w], kr_ref[(cn - 1 - t) * w:, :])
                                       + _dot(xp, qst_ref[:, t * w:(t + 1) * w]))


def s5_weights(lam_re, lam_im, b_re, b_im, c_re, c_im, d, log_dt):
    cn = S5_CHUNK
    g, p, cg = S5_GROUPS, S5_STATE, S5_GROUP
    ein = functools.partial(jnp.einsum, precision=HIGHEST)
    lr = jnp.minimum(lam_re, -1e-4)
    li = lam_im
    dt = jnp.exp(log_dt)[:, None]
    mag = jnp.exp(lr * dt)
    ar, ai = mag * jnp.cos(li * dt), mag * jnp.sin(li * dt)
    den = lr * lr + li * li
    cr = ((ar - 1.0) * lr + ai * li) / den
    ci = (ai * lr - (ar - 1.0) * li) / den
    bbr = cr[..., None] * b_re - ci[..., None] * b_im
    bbi = cr[..., None] * b_im + ci[..., None] * b_re
    def powers(taus):
        tau = jnp.asarray(taus, F32)[None, :, None]
        pmag = jnp.exp((lr * dt)[:, None, :] * tau)
        pang = (li * dt)[:, None, :] * tau
        return pmag * jnp.cos(pang), pmag * jnp.sin(pang)

    def c_times(pwr, pwi):
        return (c_re[:, None] * pwr[:, :, None, :] - c_im[:, None] * pwi[:, :, None, :],
                c_re[:, None] * pwi[:, :, None, :] + c_im[:, None] * pwr[:, :, None, :])

    dsc_r, dsc_i = powers(np.arange(cn - 1, -1, -1))
    asc_r, asc_i = powers(np.arange(1, cn + 1))
    clr, cli = c_times(dsc_r, dsc_i)
    kt = ein("gtap,gpc->gtac", clr, bbr) - ein("gtap,gpc->gtac", cli, bbi)
    is_lag0 = (jnp.arange(cn) == cn - 1).astype(F32)[None, :, None, None]
    kt = kt + is_lag0 * (d[:, :, None] * jnp.eye(cg, dtype=F32)[None])[:, None]
    kc = kt.transpose(1, 0, 3, 2).reshape(cn * g * cg, cg)
    kc = jnp.pad(kc, ((0, 0), (0, LANES - cg))).astype(BF16)
    bbr_t, bbi_t = bbr.transpose(0, 2, 1)[:, None], bbi.transpose(0, 2, 1)[:, None]
    pr = dsc_r[:, :, None, :] * bbr_t - dsc_i[:, :, None, :] * bbi_t
    pi = dsc_r[:, :, None, :] * bbi_t + dsc_i[:, :, None, :] * bbr_t
    pc = jnp.stack([pr, pi], axis=3).transpose(1, 0, 2, 3, 4).reshape(cn * g * cg, 2 * p).astype(BF16)
    clr, cli = c_times(asc_r, asc_i)
    qc = jnp.stack([clr, -cli], axis=0).transpose(0, 1, 4, 2, 3).reshape(2 * g * p, cn * cg).astype(BF16)
    lam = jnp.stack([asc_r[:, cn - 1].reshape(g * p), asc_i[:, cn - 1].reshape(g * p)], axis=0)
    return kc, pc, qc, lam


def _s5_expand_body(kc_ref, pc_ref, qc_ref, ek_ref, ep_ref, eq_ref, kr_ref, pst_ref, qst_ref):
    j = pl.program_id(1)
    g_bits, p_bits, cg_bits = 4, 6, 4
    assert (1 << g_bits, 1 << p_bits, 1 << cg_bits) == (S5_GROUPS, S5_STATE, S5_GROUP)

    def spread(src_ref, e_ref, out_ref, row_shift, col_shift):
        rows, cols = out_ref.shape[1], out_ref.shape[2]
        r = lax.broadcasted_iota(jnp.int32, (rows, cols), 0) + j * rows
        c = lax.broadcasted_iota(jnp.int32, (rows, cols), 1)
        same = (jnp.bitwise_and(jnp.right_shift(r, row_shift), S5_GROUPS - 1)
                == jnp.bitwise_and(jnp.right_shift(c, col_shift), S5_GROUPS - 1))
        out_ref[0] = jnp.where(same, _dot(src_ref[0], e_ref[...]), 0.0).astype(out_ref.dtype)

    spread(kc_ref, ek_ref, kr_ref, cg_bits, cg_bits)
    spread(pc_ref, ep_ref, pst_ref, cg_bits, p_bits)
    spread(qc_ref, eq_ref, qst_ref, p_bits, cg_bits)


def s5_expand(kc, pc, qc):
    depth = kc.shape[0]
    cn, g, p, cg = S5_CHUNK, S5_GROUPS, S5_STATE, S5_GROUP
    nblk = 8
    lane = np.arange(LANES)[:, None]
    ek = jnp.asarray(lane == (np.arange(g * cg) % cg)[None, :], BF16)
    col = np.arange(2 * g * p)[None, :]
    ep = jnp.asarray(lane == (col // (g * p)) * p + col % p, BF16)
    col = np.arange(cn * g * cg)[None, :]
    eq = jnp.asarray(np.arange(cn * cg)[:, None] == (col // (g * cg)) * cg + col % cg, BF16)
    rk, rq = cn * g * cg // nblk, 2 * g * p // nblk
    blk = lambda rows, cols: pl.BlockSpec((1, rows, cols), lambda d, j: (d, j, 0))
    const = lambda a: pl.BlockSpec(a.shape, lambda d, j: (0, 0))
    return pl.pallas_call(
        _s5_expand_body,
        grid=(depth, nblk),
        in_specs=[blk(rk, LANES), blk(rk, 2 * p), blk(rq, cn * cg), const(ek), const(ep), const(eq)],
        out_specs=[blk(rk, g * cg), blk(rk, 2 * g * p), blk(rq, cn * g * cg)],
        out_shape=[jax.ShapeDtypeStruct((depth, cn * g * cg, g * cg), BF16),
                   jax.ShapeDtypeStruct((depth, cn * g * cg, 2 * g * p), BF16),
                   jax.ShapeDtypeStruct((depth, 2 * g * p, cn * g * cg), BF16)],
        compiler_params=_cparams("parallel", "parallel"),
        name="s5_expand",
    )(kc, pc, qc, ek, ep, eq)


def s5_scan(su, weights, batch):
    t, w = su.shape
    cn = S5_CHUNK
    kr, pst, qst, lam = weights
    nr = t // cn
    nch = nr // batch
    nseq = 2
    rows = nseq * nch
    u = su.reshape(nr, cn * w)
    ns = pst.shape[1]
    once = lambda a: pl.BlockSpec(a.shape, lambda i: (0, 0), pipeline_mode=pl.Buffered(1))
    rowb = lambda width: pl.BlockSpec((rows, width), lambda i: (i, 0))
    xp = pl.pallas_call(
        functools.partial(_s5_state_body, nch=nch, nseq=nseq),
        grid=(nr // rows,),
        in_specs=[rowb(cn * w), once(pst), once(lam)],
        out_specs=rowb(ns),
        out_shape=jax.ShapeDtypeStruct((nr, ns), BF16),
        scratch_shapes=[pltpu.VMEM((rows, ns), F32)] * 2,
        compiler_params=_cparams("parallel"),
        name="s5_state",
    )(u, pst, lam)
    y = pl.pallas_call(
        _s5_out_body,
        grid=(nr // rows,),
        in_specs=[rowb(cn * w), rowb(ns), once(kr), once(qst)],
        out_specs=rowb(cn * w),
        out_shape=jax.ShapeDtypeStruct((nr, cn * w), F32),
        compiler_params=_cparams("parallel"),
        name="s5_out",
    )(u, xp, kr, qst)
    return y.reshape(t, w)


def _mix_ffn_body(x_ref, oa_ref, ob_ref, oc_ref, ys_ref, xh_ref, oah_ref, obh_ref, och_ref, ysh_ref,
                  gw_ref, gb_ref, wo_ref, g_ref, wup_ref, cw_ref, wdn_ref, fg_ref, o_ref, up_scr,
                  *, tm, fc, final_norm):
    i = pl.program_id(1)
    keep = (i > 0).astype(F32)

    def norm(v, g):
        return v * lax.rsqrt(jnp.mean(v * v, axis=-1, keepdims=True) + EPS) * g

    def mixed(xr, oa, ob, oc, ys):
        y = jax.nn.gelu(ys)
        od = y * _sigmoid(_dot(y, gw_ref[...]) + gb_ref[...])
        mix = jnp.concatenate([oa, ob, oc, od.astype(BF16)], axis=1)
        return xr + _dot(mix, wo_ref[...])

    x = mixed(x_ref[0], oa_ref[0], ob_ref[0], oc_ref[0], ys_ref[0])
    x_halo = mixed(xh_ref[0], oah_ref[0], obh_ref[0], och_ref[0], ysh_ref[0])
    h_main = norm(x, g_ref[...]).astype(BF16)
    h_halo = norm(x_halo, g_ref[...]).astype(BF16)
    acc = x
    for c in range(D_FF // fc):
        acts = []
        for part in range(2):
            lo = part * D_FF + c * fc
            up_scr[0:SUBLANES, :] = _dot(h_halo, wup_ref[:, lo:lo + fc]) * keep
            up_scr[SUBLANES:, :] = _dot(h_main, wup_ref[:, lo:lo + fc])
            w = cw_ref[:, lo:lo + fc]
            y = w[FFN_CONV - 1:FFN_CONV, :] * up_scr[pl.ds(SUBLANES, tm), :]
            for tap in range(FFN_CONV - 1):
                y = y + w[tap:tap + 1, :] * up_scr[pl.ds(SUBLANES - (FFN_CONV - 1) + tap, tm), :]
            acts.append(y)
        act = _silu(acts[0]) * acts[1]
        acc = acc + _dot(act, wdn_ref[c * fc:(c + 1) * fc, :])
    o_ref[0] = norm(acc, fg_ref[...]) if final_norm else acc


def mix_ffn(x3, mixers, glu_w, glu_b, w_out, norm_g, w_up, conv_w, w_down, final_g, tm, fc):
    b, l, dm = x3.shape
    halo_blocks = tm // SUBLANES
    once = lambda a: pl.BlockSpec(a.shape, lambda bi, i: (0, 0), pipeline_mode=pl.Buffered(1))
    main = lambda a: pl.BlockSpec((1, tm, a.shape[2]), lambda bi, i: (bi, i, 0))
    halo = lambda a: pl.BlockSpec((1, SUBLANES, a.shape[2]),
                                  lambda bi, i: (bi, jnp.maximum(i * halo_blocks - 1, 0), 0))
    g = norm_g.reshape(1, dm)
    gb = glu_b.reshape(1, -1)
    final_norm = final_g is not None
    fg = final_g.reshape(1, dm) if final_norm else g
    acts = (x3,) + tuple(mixers)
    return pl.pallas_call(
        functools.partial(_mix_ffn_body, tm=tm, fc=fc, final_norm=final_norm),
        grid=(b, l // tm),
        in_specs=[main(a) for a in acts] + [halo(a) for a in acts]
        + [once(a) for a in (glu_w, gb, w_out, g, w_up, conv_w, w_down, fg)],
        out_specs=pl.BlockSpec((1, tm, dm), lambda bi, i: (bi, i, 0)),
        out_shape=jax.ShapeDtypeStruct((b, l, dm), F32),
        scratch_shapes=[pltpu.VMEM((tm + SUBLANES, fc), F32)],
        compiler_params=_cparams("parallel", "arbitrary"),
        name="mix_ffn",
    )(*acts, *acts, glu_w, gb, w_out, g, w_up, conv_w, w_down, fg)


def _permute_w_in(w_in):
    dm = w_in.shape[0]
    gw, mw = GDN_HEADS * GDN_DK, MOBA_HEADS * MOBA_HD
    o_aq, o_az, o_aa, o_ab = 0, 3 * gw, 4 * gw, 4 * gw + GDN_HEADS
    o_mq = o_ab + GDN_HEADS
    o_cq = o_mq + 3 * mw
    o_ckv = o_cq + MLA_Q_LORA
    o_ckr = o_ckv + MLA_KV_LORA
    o_su = o_ckr + MLA_ROPE
    hr = MLA_ROPE // 2
    z = lambda n: jnp.zeros((dm, n), w_in.dtype)
    ckr = w_in[:, o_ckr:o_ckr + MLA_ROPE]
    ckr_swap = jnp.concatenate([-ckr[:, hr:], ckr[:, :hr]], axis=1)
    cols = [w_in[:, o_aq:o_aq + 3 * gw],
            w_in[:, o_mq:o_mq + 3 * mw],
            w_in[:, o_az:o_az + gw],
            w_in[:, o_su:o_su + S5_WIDTH],
            w_in[:, o_cq:o_cq + MLA_Q_LORA], z(COL_CKV - COL_CQ - MLA_Q_LORA),
            w_in[:, o_ckv:o_ckv + MLA_KV_LORA],
            w_in[:, o_aa:o_aa + 2 * GDN_HEADS], z(MLA_NOPE - 2 * GDN_HEADS), ckr, z(LANES - MLA_NOPE - MLA_ROPE),
            z(MLA_NOPE), ckr_swap, z(LANES - MLA_NOPE - MLA_ROPE)]
    return jnp.concatenate(cols, axis=1).astype(BF16)


def _layer(x3, tables, w, final_g):
    b, l, dm = x3.shape
    t = b * l
    x2 = x3.reshape(t, dm)
    proj, su = in_proj(x2, w["norm1_g"], w["w_in"], ROW_TILE)
    proj3 = proj.reshape(b, l, N_PROJ)

    q, k, v, beta, gam, gr = gdn_prep(proj3, w["gdn_conv_w"], w["gdn_alog"], w["gdn_dtb"], ROW_TILE)
    o_a = gdn_chunk(q, k, v, beta, gam, gr, proj3, w["gdn_norm_g"], ROW_TILE)

    mq, mk, mv = moba_gate(proj3)
    o_b = flash_attention(mq, mk, mv, nh=MOBA_HEADS, name="flash_moba")

    cq, ck, cv = mla_proj(proj3, w["mla"], tables, ROW_TILE)
    o_c = flash_attention(cq, ck, cv, nh=MLA_HEADS, name="flash_mla")

    ys = s5_scan(su, w["s5"], b).reshape(b, l, S5_WIDTH)

    return mix_ffn(x3, (o_a, o_b, o_c, ys), w["s5_glu_w"], w["s5_glu_b"], w["w_out"], w["norm2_g"],
                   w["ffn_w_up"], w["ffn_conv_w"], w["ffn_w_down"], final_g, FFN_ROW_TILE, D_FF // 2)


def _prepare_weights(p):
    nh = GDN_HEADS
    return {
        "norm1_g": p["norm1_g"], "w_in": _permute_w_in(p["w_in"]),
        "gdn_conv_w": p["gdn_conv_w"], "gdn_norm_g": p["gdn_norm_g"],
        "gdn_alog": jnp.pad(p["gdn_a_log"], (0, LANES - nh)).reshape(1, LANES),
        "gdn_dtb": jnp.pad(p["gdn_dt_bias"], (0, LANES - nh)).reshape(1, LANES),
        "mla": mla_weights(p["mla_q_norm_g"], p["mla_w_uq"], p["mla_kv_norm_g"], p["mla_w_ukv"]),
        "s5": s5_weights(p["s5_lam_re"], p["s5_lam_im"], p["s5_b_re"], p["s5_b_im"], p["s5_c_re"], p["s5_c_im"],
                         p["s5_d"], p["s5_log_dt"]),
        "s5_glu_w": p["s5_glu_w"].astype(BF16), "s5_glu_b": p["s5_glu_b"], "w_out": p["w_out"].astype(BF16),
        "norm2_g": p["norm2_g"], "ffn_w_up": p["ffn_w_up"].astype(BF16), "ffn_conv_w": p["ffn_conv_w"],
        "ffn_w_down": p["ffn_w_down"].astype(BF16),
    }


_LAYER_KEYS = ("norm1_g", "w_in", "gdn_conv_w", "gdn_a_log", "gdn_dt_bias", "gdn_norm_g",
               "mla_q_norm_g", "mla_w_uq", "mla_kv_norm_g", "mla_w_ukv",
               "s5_lam_re", "s5_lam_im", "s5_b_re", "s5_b_im", "s5_c_re", "s5_c_im", "s5_d", "s5_log_dt",
               "s5_glu_w", "s5_glu_b", "w_out", "norm2_g", "ffn_w_up", "ffn_conv_w", "ffn_w_down")


def kernel(x, norm1_g, w_in, gdn_conv_w, gdn_a_log, gdn_dt_bias, gdn_norm_g, mla_q_norm_g, mla_w_uq,
           mla_kv_norm_g, mla_w_ukv, s5_lam_re, s5_lam_im, s5_b_re, s5_b_im, s5_c_re, s5_c_im, s5_d,
           s5_log_dt, s5_glu_w, s5_glu_b, w_out, norm2_g, ffn_w_up, ffn_conv_w, ffn_w_down, final_norm_g):
    stacked = (norm1_g, w_in, gdn_conv_w, gdn_a_log, gdn_dt_bias, gdn_norm_g, mla_q_norm_g, mla_w_uq,
               mla_kv_norm_g, mla_w_ukv, s5_lam_re, s5_lam_im, s5_b_re, s5_b_im, s5_c_re, s5_c_im, s5_d,
               s5_log_dt, s5_glu_w, s5_glu_b, w_out, norm2_g, ffn_w_up, ffn_conv_w, ffn_w_down)
    l = x.shape[1]
    inv_freq = ROPE_THETA ** (-jnp.arange(0, MLA_ROPE, 2, dtype=F32) / MLA_ROPE)
    ang = jnp.arange(l, dtype=F32)[:, None] * inv_freq[None, :]
    tables = mla_tables(jnp.cos(ang), jnp.sin(ang))
    weights = jax.vmap(_prepare_weights)(dict(zip(_LAYER_KEYS, stacked)))
    kc, pc, qc, lam = weights["s5"]
    weights["s5"] = (*s5_expand(kc, pc, qc), lam)
    depth = norm1_g.shape[0]
    for i in range(depth):
        w = jax.tree_util.tree_map(lambda a: a[i], weights)
        x = _layer(x, tables, w, final_norm_g if i == depth - 1 else None)
    return x
```
